```python
import math
import jax
import jax.numpy as jnp
from jax import lax
import numpy as np

D_MODEL = 1024
BATCH = 8
SEQ = 2048
DEPTH = 1
DEC_BATCH = 128
DEC_SEQ = 8
PAST_LEN = 16384
PAGE_SIZE = 128

RW_HEAD = 64
RW_HEADS = D_MODEL // 128
RW_DIM = RW_HEADS * RW_HEAD
RW_DECAY_LORA = 64
RW_A_LORA = 64
RW_GATE_LORA = 128
RW_LN_EPS = 64e-5
RW_COLS = 3 * RW_DIM + RW_DECAY_LORA + RW_A_LORA + RW_GATE_LORA
SSM_DIM = D_MODEL
SSM_HEAD = 64
SSM_HEADS = SSM_DIM // SSM_HEAD
SSM_GROUPS = 2
SSM_STATE = 128
SSM_CONV = 4
SSM_CHUNK = 128
SSM_CONV_DIM = SSM_DIM + 2 * SSM_GROUPS * SSM_STATE
SSM_COLS = SSM_DIM + SSM_CONV_DIM + SSM_HEADS
GATE_COLS = 2 * D_MODEL
IN_COLS = RW_COLS + SSM_COLS + GATE_COLS
PEER_HEADS = 8
PEER_NKEYS = 128
PEER_EXPERTS = PEER_NKEYS * PEER_NKEYS
PEER_TOPK = 16
PEER_DKEY = 256
PEER_BLOCK = 128
NORM_EPS = 1e-6

kernel_name = 'rwkv7_mamba2_peer_hybrid_step'


def _rmsnorm(x, g):
    xf = x.astype(jnp.float32)
    y = xf * lax.rsqrt(jnp.mean(xf * xf, axis=-1, keepdims=True) + NORM_EPS)
    return (y * g.astype(jnp.float32)).astype(x.dtype)


def _seg_decay(a):
    L = a.shape[-1]
    cs = jnp.cumsum(a, axis=-1)
    seg = cs[..., :, None] - cs[..., None, :]
    mask = jnp.tril(jnp.ones((L, L), dtype=bool))
    return jnp.exp(jnp.where(mask, seg, -jnp.inf))


def _rwkv7(p, p_prev, s0, mu, w0, w2, a0, a2, g2, k_k, k_a, r_k, ln_w, ln_b):
    f32 = jnp.float32
    B, T, _ = p.shape
    prev = jnp.concatenate([p_prev[:, None], p[:, :-1]], axis=1)
    q = p + (prev - p) * mu
    o = 0
    r = q[..., o:o + RW_DIM]; o += RW_DIM
    k = q[..., o:o + RW_DIM]; o += RW_DIM
    v = q[..., o:o + RW_DIM]; o += RW_DIM
    lw = q[..., o:o + RW_DECAY_LORA]; o += RW_DECAY_LORA
    la = q[..., o:o + RW_A_LORA]; o += RW_A_LORA
    lg = q[..., o:o + RW_GATE_LORA]
    w = -jax.nn.softplus(-(w0 + jnp.tanh(lw) @ w2).astype(f32)) - 0.5
    decay = jnp.exp(-jnp.exp(w))
    a = jax.nn.sigmoid((a0 + la @ a2).astype(f32))
    g = (jax.nn.sigmoid(lg) @ g2).astype(f32)
    hs = lambda t: t.reshape(B, T, RW_HEADS, RW_HEAD).astype(f32)
    r, k, v, decay, a = hs(r), hs(k), hs(v), hs(decay), hs(a)
    kk = k * k_k.reshape(RW_HEADS, RW_HEAD).astype(f32)
    kk = kk / jnp.maximum(jnp.sqrt(jnp.sum(kk * kk, axis=-1, keepdims=True)), 1e-12)
    k = k * (1.0 + (a - 1.0) * k_a.reshape(RW_HEADS, RW_HEAD).astype(f32))

    def step(S, inp):
        r_t, d_t, k_t, v_t, kk_t, b_t = inp
        sk = jnp.einsum('bhvk,bhk->bhv', S, kk_t)
        S = S * d_t[:, :, None, :] - sk[..., None] * b_t[:, :, None, :] + v_t[..., None] * k_t[:, :, None, :]
        return S, jnp.einsum('bhvk,bhk->bhv', S, r_t)

    tm = lambda t: jnp.moveaxis(t, 1, 0)
    sT, y = lax.scan(step, s0.astype(f32), (tm(r), tm(decay), tm(k), tm(v), tm(kk), tm(kk * a)))
    y = jnp.moveaxis(y, 0, 1)
    mean = jnp.mean(y, axis=-1, keepdims=True)
    var = jnp.mean(jnp.square(y - mean), axis=-1, keepdims=True)
    yn = ((y - mean) * lax.rsqrt(var + RW_LN_EPS)).reshape(B, T, RW_DIM)
    yn = yn * ln_w.astype(f32) + ln_b.astype(f32)
    bonus = (jnp.sum(r * k * r_k.astype(f32), axis=-1, keepdims=True) * v).reshape(B, T, RW_DIM)
    out = ((yn + bonus) * g).astype(p.dtype)
    return out, sT.astype(s0.dtype)


def _ssd(X, dA, Bm, Cm, h0):
    Bsz, T, H, P = X.shape
    G, N = Bm.shape[2], Bm.shape[3]
    J = H // G
    L = math.gcd(T, SSM_CHUNK)
    nc = T // L
    X = X.reshape(Bsz, nc, L, G, J, P)
    a = jnp.transpose(dA.reshape(Bsz, nc, L, G, J), (0, 3, 4, 1, 2))
    Bc = Bm.reshape(Bsz, nc, L, G, N)
    Cc = Cm.reshape(Bsz, nc, L, G, N)
    cs = jnp.cumsum(a, axis=-1)
    cb = jnp.einsum('bclgn,bcsgn->bgcls', Cc, Bc)
    y_diag = jnp.einsum('bgcls,bgjcls,bcsgjp->bclgjp', cb, _seg_decay(a), X)
    decay_states = jnp.exp(cs[..., -1:] - cs)
    states = jnp.einsum('bclgn,bgjcl,bclgjp->bcgjpn', Bc, decay_states, X)
    states = jnp.concatenate([h0.reshape(Bsz, 1, G, J, P, N), states], axis=1)
    chunk_a = jnp.pad(cs[..., -1], ((0, 0), (0, 0), (0, 0), (1, 0)))
    new_states = jnp.einsum('bgjzc,bcgjpn->bzgjpn', _seg_decay(chunk_a), states)
    y_off = jnp.einsum('bclgn,bcgjpn,bgjcl->bclgjp', Cc, new_states[:, :-1], jnp.exp(cs))
    y = (y_diag + y_off).reshape(Bsz, T, H, P)
    return y, new_states[:, -1].reshape(Bsz, H, P, N)


def _mamba2(p, conv_buf, h0, conv_w, conv_b, dt_bias, A_log, D_skip, norm_w):
    f32 = jnp.float32
    B, T, _ = p.shape
    z = p[..., :SSM_DIM]
    xbc = p[..., SSM_DIM:SSM_DIM + SSM_CONV_DIM]
    dt = p[..., SSM_DIM + SSM_CONV_DIM:]
    full = jnp.concatenate([conv_buf.astype(p.dtype), xbc], axis=1)
    new_buf = full[:, -(SSM_CONV - 1):]
    conv = conv_b + full[:, 0:T] * conv_w[0]
    for j in range(1, SSM_CONV):
        conv = conv + full[:, j:j + T] * conv_w[j]
    xbc = jax.nn.silu(conv)
    gn = SSM_GROUPS * SSM_STATE
    xs = xbc[..., :SSM_DIM].reshape(B, T, SSM_HEADS, SSM_HEAD).astype(f32)
    Bm = xbc[..., SSM_DIM:SSM_DIM + gn].reshape(B, T, SSM_GROUPS, SSM_STATE).astype(f32)
    Cm = xbc[..., SSM_DIM + gn:].reshape(B, T, SSM_GROUPS, SSM_STATE).astype(f32)
    dt = jax.nn.softplus((dt + dt_bias).astype(f32))
    A = -jnp.exp(A_log.astype(f32))
    y, hT = _ssd(xs * dt[..., None], dt * A, Bm, Cm, h0.astype(f32))
    y = y + xs * D_skip.astype(f32)[:, None]
    y = y.reshape(B, T, SSM_DIM) * jax.nn.silu(z.astype(f32))
    yg = y.reshape(B, T, SSM_GROUPS, SSM_DIM // SSM_GROUPS)
    yg = yg * lax.rsqrt(jnp.mean(yg * yg, axis=-1, keepdims=True) + NORM_EPS)
    out = (yg.reshape(B, T, SSM_DIM) * norm_w.astype(f32)).astype(p.dtype)
    return out, new_buf.astype(conv_buf.dtype), hT.astype(h0.dtype)


def _peer(xn, wq, keys, u, v):
    B, T, D = xn.shape
    n = B * T
    n_pad = -(-n // PEER_BLOCK) * PEER_BLOCK
    xt = jnp.pad(xn.reshape(n, D), ((0, n_pad - n), (0, 0))).reshape(n_pad // PEER_BLOCK, PEER_BLOCK, D)

    def block(xb):
        q = (xb @ wq).reshape(PEER_BLOCK, PEER_HEADS, 2, PEER_DKEY // 2)
        s = jnp.einsum('thpd,hpkd->thpk', q, keys).astype(jnp.float32)
        s1, i1 = lax.top_k(s[:, :, 0], PEER_TOPK)
        s2, i2 = lax.top_k(s[:, :, 1], PEER_TOPK)
        cand = (s1[..., :, None] + s2[..., None, :]).reshape(PEER_BLOCK, PEER_HEADS, PEER_TOPK * PEER_TOPK)
        sc, ic = lax.top_k(cand, PEER_TOPK)
        e1 = jnp.take_along_axis(i1, ic // PEER_TOPK, axis=-1)
        e2 = jnp.take_along_axis(i2, ic % PEER_TOPK, axis=-1)
        idx = e1 * PEER_NKEYS + e2
        gate = jax.nn.softmax(sc, axis=-1)
        act = jax.nn.gelu(jnp.einsum('td,thkd->thk', xb, u[idx]).astype(jnp.float32), approximate=False)
        return jnp.einsum('thk,thkd->td', (gate * act).astype(xb.dtype), v[idx])

    out = lax.map(block, xt)
    return out.reshape(n_pad, D)[:n].reshape(B, T, D)


def _layer(x, c, shift0, wkv0, conv0, ssm0, lp):
    f32 = jnp.float32
    mod = jax.nn.silu(c) @ lp['w_ada'] + lp['b_ada']
    sh1, sc1, gt1, sh2, sc2, gt2 = jnp.split(mod, 6, axis=-1)
    xn = _rmsnorm(x, lp['norm1_g']) * (1 + sc1[:, None]) + sh1[:, None]
    proj = xn @ lp['w_in']
    p_rw = proj[..., :RW_COLS]
    p_ssm = proj[..., RW_COLS:RW_COLS + SSM_COLS]
    p_gate = proj[..., RW_COLS + SSM_COLS:]
    p_prev = shift0.astype(x.dtype) @ lp['w_in'][:, :RW_COLS]
    o_a, wkvT = _rwkv7(p_rw, p_prev, wkv0, lp['rw_mu'], lp['rw_w0'], lp['rw_w2'], lp['rw_a0'], lp['rw_a2'],
                       lp['rw_g2'], lp['rw_k_k'], lp['rw_k_a'], lp['rw_r_k'], lp['rw_ln_w'], lp['rw_ln_b'])
    o_b, convT, ssmT = _mamba2(p_ssm, conv0, ssm0, lp['ssm_conv_w'], lp['ssm_conv_b'], lp['ssm_dt_bias'],
                               lp['ssm_A_log'], lp['ssm_D'], lp['ssm_norm_w'])
    g = jax.nn.sigmoid(p_gate.astype(f32)).astype(x.dtype)
    merged = g[..., :D_MODEL] * (o_a @ lp['w_pa']) + g[..., D_MODEL:] * (o_b @ lp['w_pb'])
    x = x + gt1[:, None] * (merged @ lp['w_out'])
    xn2 = _rmsnorm(x, lp['norm2_g']) * (1 + sc2[:, None]) + sh2[:, None]
    x = x + gt2[:, None] * _peer(xn2, lp['peer_wq'], lp['peer_keys'], lp['peer_u'], lp['peer_v'])
    return x, xn[:, -1].astype(shift0.dtype), wkvT, convT, ssmT


def _trunk(x, c, shift, wkv, conv, ssm, params, final_g, w_ada_f, b_ada_f):
    n_sh, n_wkv, n_conv, n_ssm = [], [], [], []
    for l in range(DEPTH):
        lp = {name: arr[l] for name, arr in params.items()}
        x, s_sh, s_wkv, s_conv, s_ssm = _layer(x, c, shift[l], wkv[l], conv[l], ssm[l], lp)
        n_sh.append(s_sh); n_wkv.append(s_wkv); n_conv.append(s_conv); n_ssm.append(s_ssm)
    fm = jax.nn.silu(c) @ w_ada_f + b_ada_f
    shf, scf = jnp.split(fm, 2, axis=-1)
    y = _rmsnorm(x, final_g) * (1 + scf[:, None]) + shf[:, None]
    return y, jnp.stack(n_sh), jnp.stack(n_wkv), jnp.stack(n_conv), jnp.stack(n_ssm)


def setup_inputs(seed: int = 0) -> dict:
    key = jax.random.key(seed)
    ks = iter(jax.random.split(key, 48))
    f32 = jnp.float32
    nrm = lambda shape, scale: jax.random.normal(next(ks), shape, f32) * scale
    uni = lambda shape, lo, hi: jax.random.uniform(next(ks), shape, f32, lo, hi)
    L = DEPTH
    dt0 = jnp.exp(uni((L, SSM_HEADS), math.log(1e-3), math.log(1e-1)))
    return {
        'x_prompt': nrm((BATCH, SEQ, D_MODEL), 1.0),
        'x_sample': nrm((DEC_BATCH, DEC_SEQ, D_MODEL), 1.0),
        'c_prompt': nrm((BATCH, D_MODEL), 1.0),
        'c_sample': nrm((DEC_BATCH, D_MODEL), 1.0),
        'state_shift': nrm((L, DEC_BATCH, D_MODEL), 1.0),
        'state_wkv': nrm((L, DEC_BATCH, RW_HEADS, RW_HEAD, RW_HEAD), 0.3),
        'state_conv': nrm((L, DEC_BATCH, SSM_CONV - 1, SSM_CONV_DIM), 1.0),
        'state_ssm': nrm((L, DEC_BATCH, SSM_HEADS, SSM_HEAD, SSM_STATE), 0.1),
        'w_ada': nrm((L, D_MODEL, 6 * D_MODEL), 0.5 * D_MODEL ** -0.5),
        'b_ada': nrm((L, 6 * D_MODEL), 0.01),
        'norm1_g': 1.0 + nrm((L, D_MODEL), 0.02),
        'w_in': nrm((L, D_MODEL, IN_COLS), D_MODEL ** -0.5),
        'rw_mu': uni((L, RW_COLS), 0.0, 1.0),
        'rw_w0': uni((L, RW_DIM), -3.0, 0.0),
        'rw_w2': nrm((L, RW_DECAY_LORA, RW_DIM), 0.5 * RW_DECAY_LORA ** -0.5),
        'rw_a0': nrm((L, RW_DIM), 0.1),
        'rw_a2': nrm((L, RW_A_LORA, RW_DIM), 0.5 * RW_A_LORA ** -0.5),
        'rw_g2': nrm((L, RW_GATE_LORA, RW_DIM), RW_GATE_LORA ** -0.5),
        'rw_k_k': 0.85 + nrm((L, RW_DIM), 0.1),
        'rw_k_a': 1.0 + nrm((L, RW_DIM), 0.1),
        'rw_r_k': nrm((L, RW_HEADS, RW_HEAD), 0.1),
        'rw_ln_w': 1.0 + nrm((L, RW_DIM), 0.02),
        'rw_ln_b': nrm((L, RW_DIM), 0.01),
        'ssm_conv_w': nrm((L, SSM_CONV, SSM_CONV_DIM), 0.5),
        'ssm_conv_b': nrm((L, SSM_CONV_DIM), 0.01),
        'ssm_dt_bias': dt0 + jnp.log(-jnp.expm1(-dt0)),
        'ssm_A_log': jnp.log(uni((L, SSM_HEADS), 1.0, 16.0)),
        'ssm_D': 1.0 + nrm((L, SSM_HEADS), 0.1),
        'ssm_norm_w': 1.0 + nrm((L, SSM_DIM), 0.02),
        'w_pa': nrm((L, RW_DIM, D_MODEL), RW_DIM ** -0.5),
        'w_pb': nrm((L, SSM_DIM, D_MODEL), SSM_DIM ** -0.5),
        'w_out': nrm((L, D_MODEL, D_MODEL), D_MODEL ** -0.5),
        'norm2_g': 1.0 + nrm((L, D_MODEL), 0.02),
        'peer_wq': nrm((L, D_MODEL, PEER_HEADS * PEER_DKEY), D_MODEL ** -0.5),
        'peer_keys': nrm((L, PEER_HEADS, 2, PEER_NKEYS, PEER_DKEY // 2), (PEER_DKEY // 2) ** -0.5),
        'peer_u': nrm((L, PEER_EXPERTS, D_MODEL), D_MODEL ** -0.5),
        'peer_v': nrm((L, PEER_EXPERTS, D_MODEL), PEER_HEADS ** -0.5),
        'final_g': 1.0 + nrm((D_MODEL,), 0.02),
        'w_ada_f': nrm((D_MODEL, 2 * D_MODEL), 0.5 * D_MODEL ** -0.5),
        'b_ada_f': nrm((2 * D_MODEL,), 0.01),
    }


def reference(x_prompt, x_sample, c_prompt, c_sample, state_shift, state_wkv, state_conv, state_ssm,
              w_ada, b_ada, norm1_g, w_in, rw_mu, rw_w0, rw_w2, rw_a0, rw_a2, rw_g2, rw_k_k, rw_k_a, rw_r_k,
              rw_ln_w, rw_ln_b, ssm_conv_w, ssm_conv_b, ssm_dt_bias, ssm_A_log, ssm_D, ssm_norm_w,
              w_pa, w_pb, w_out, norm2_g, peer_wq, peer_keys, peer_u, peer_v, final_g, w_ada_f, b_ada_f):
    params = {
        'w_ada': w_ada, 'b_ada': b_ada, 'norm1_g': norm1_g, 'w_in': w_in,
        'rw_mu': rw_mu, 'rw_w0': rw_w0, 'rw_w2': rw_w2, 'rw_a0': rw_a0, 'rw_a2': rw_a2, 'rw_g2': rw_g2,
        'rw_k_k': rw_k_k, 'rw_k_a': rw_k_a, 'rw_r_k': rw_r_k, 'rw_ln_w': rw_ln_w, 'rw_ln_b': rw_ln_b,
        'ssm_conv_w': ssm_conv_w, 'ssm_conv_b': ssm_conv_b, 'ssm_dt_bias': ssm_dt_bias,
        'ssm_A_log': ssm_A_log, 'ssm_D': ssm_D, 'ssm_norm_w': ssm_norm_w,
        'w_pa': w_pa, 'w_pb': w_pb, 'w_out': w_out, 'norm2_g': norm2_g,
        'peer_wq': peer_wq, 'peer_keys': peer_keys, 'peer_u': peer_u, 'peer_v': peer_v,
    }
    Bp = x_prompt.shape[0]
    dt = x_prompt.dtype
    z_shift = jnp.zeros((DEPTH, Bp, D_MODEL), dt)
    z_wkv = jnp.zeros((DEPTH, Bp, RW_HEADS, RW_HEAD, RW_HEAD), dt)
    z_conv = jnp.zeros((DEPTH, Bp, SSM_CONV - 1, SSM_CONV_DIM), dt)
    z_ssm = jnp.zeros((DEPTH, Bp, SSM_HEADS, SSM_HEAD, SSM_STATE), dt)
    y_prompt, p_shift, p_wkv, p_conv, p_ssm = _trunk(x_prompt, c_prompt, z_shift, z_wkv, z_conv, z_ssm,
                                                     params, final_g, w_ada_f, b_ada_f)
    y_sample, s_shift, s_wkv, s_conv, s_ssm = _trunk(x_sample, c_sample, state_shift, state_wkv, state_conv,
                                                     state_ssm, params, final_g, w_ada_f, b_ada_f)
    return (y_prompt, y_sample, p_shift, p_wkv, p_conv, p_ssm, s_shift, s_wkv, s_conv, s_ssm)
```

```python
import functools
import math

import jax
import jax.numpy as jnp
from jax import lax
from jax.experimental import pallas as pl
from jax.experimental.pallas import tpu as pltpu

f32 = jnp.float32
bf16 = jnp.bfloat16
HI = lax.Precision.HIGHEST

D_MODEL = 1024
RW_HEAD = 64
RW_HEADS = 8
RW_DIM = RW_HEADS * RW_HEAD
RW_COLS = 3 * RW_DIM + 64 + 64 + 128
RW_LN_EPS = 64e-5
SSM_DIM = 1024
SSM_HEAD = 64
SSM_HEADS = 16
SSM_GROUPS = 2
SSM_STATE = 128
SSM_CONV = 4
SSM_CHUNK = 128
SSM_CONV_DIM = SSM_DIM + 2 * SSM_GROUPS * SSM_STATE
PEER_HEADS = 8
PEER_NKEYS = 128
PEER_TOPK = 16
PEER_DKEY = 256
NORM_EPS = 1e-6

LANES = 128
VMEM_LIMIT = 56 * 1024 * 1024
NEG = -1e30
INV_SQRT2 = 0.7071067811865476

PEER_TM = 512
PEER_EB = 1024
SCAN_BB = 8


def _cparams(sem):
    return pltpu.CompilerParams(dimension_semantics=sem, vmem_limit_bytes=VMEM_LIMIT)


def _softplus(x):
    return jnp.maximum(x, 0.0) + jnp.log1p(jnp.exp(-jnp.abs(x)))


def _sigmoid(x):
    return jax.nn.sigmoid(x)


def _nt(a, b):
    return lax.dot_general(a, b, (((1,), (1,)), ((), ())), preferred_element_type=f32)


def _tn(a, b):
    return lax.dot_general(a, b, (((0,), (0,)), ((), ())), preferred_element_type=f32)


def _tile_bt(B, T):
    if T >= 512:
        return 1, 512
    bb = max(1, min(B, 256 // T))
    return bb, T


def _ada_body(c_ref, w_ref, b_ref, o_ref):
    c = c_ref[...]
    s = c * _sigmoid(c)
    o_ref[...] = jnp.dot(s, w_ref[...], precision=HI, preferred_element_type=f32) + b_ref[...]


def _ada(c, w, b):
    M, Dm = c.shape
    Nc = w.shape[1]
    tn = 1024
    return pl.pallas_call(
        _ada_body,
        grid=(Nc // tn,),
        in_specs=[pl.BlockSpec((M, Dm), lambda j: (0, 0)),
                  pl.BlockSpec((Dm, tn), lambda j: (0, j)),
                  pl.BlockSpec((1, tn), lambda j: (0, j))],
        out_specs=pl.BlockSpec((M, tn), lambda j: (0, j)),
        out_shape=jax.ShapeDtypeStruct((M, Nc), f32),
        compiler_params=_cparams(("arbitrary",)),
        name="ada",
    )(c, w, b)


def _nm_body(x_ref, sc_ref, sh_ref, g_ref, *refs, n_w):
    w_refs, o_refs = refs[:n_w], refs[n_w:]
    x = x_ref[...]
    bb, tt, Dm = x.shape
    ms = jnp.mean(x * x, axis=-1, keepdims=True)
    xn = x * lax.rsqrt(ms + NORM_EPS) * g_ref[...]
    xn = xn * (1.0 + sc_ref[...]) + sh_ref[...]
    xb = xn.reshape(bb * tt, Dm).astype(bf16)
    for w_ref, o_ref in zip(w_refs, o_refs):
        o_ref[...] = jnp.dot(xb, w_ref[...], preferred_element_type=f32)


def _nm_matmul(x3, sc, sh, g, ws, name):
    B, T, Dm = x3.shape
    bb, tt = _tile_bt(B, T)
    nt = T // tt
    rows = bb * tt
    in_specs = [pl.BlockSpec((bb, tt, Dm), lambda i, j: (i, j, 0)),
                pl.BlockSpec((bb, 1, Dm), lambda i, j: (i, 0, 0)),
                pl.BlockSpec((bb, 1, Dm), lambda i, j: (i, 0, 0)),
                pl.BlockSpec((1, 1, Dm), lambda i, j: (0, 0, 0))]
    in_specs += [pl.BlockSpec(w.shape, lambda i, j: (0, 0)) for w in ws]
    out_specs = [pl.BlockSpec((rows, w.shape[1]), lambda i, j: (i * nt + j, 0)) for w in ws]
    out_shape = [jax.ShapeDtypeStruct((B * T, w.shape[1]), f32) for w in ws]
    return pl.pallas_call(
        functools.partial(_nm_body, n_w=len(ws)),
        grid=(B // bb, nt),
        in_specs=in_specs, out_specs=out_specs, out_shape=out_shape,
        compiler_params=_cparams(("arbitrary", "arbitrary")),
        name=name,
    )(x3, sc, sh, g, *ws)


def _xnlast_body(x_ref, sc_ref, sh_ref, g_ref, o_ref):
    x = x_ref[...]
    ms = jnp.mean(x * x, axis=-1, keepdims=True)
    xn = x * lax.rsqrt(ms + NORM_EPS) * g_ref[...]
    o_ref[...] = xn * (1.0 + sc_ref[...]) + sh_ref[...]


def _xnlast(x2, sc, sh, g):
    return pl.pallas_call(_xnlast_body, out_shape=jax.ShapeDtypeStruct(x2.shape, f32), name="xn_last")(x2, sc, sh, g)


def _mm_body(x_ref, w_ref, o_ref):
    o_ref[...] = jnp.dot(x_ref[...].astype(bf16), w_ref[...], preferred_element_type=f32)


def _mm(x, w):
    return pl.pallas_call(_mm_body, out_shape=jax.ShapeDtypeStruct((x.shape[0], w.shape[1]), f32),
                          compiler_params=_cparams(None), name="shift_proj")(x, w)


def _rwpre_body(p_ref, prev_ref, mu_ref, w0_ref, w2_ref, a0_ref, a2_ref, g2_ref, kk_ref_, ka_ref, rk_ref, ones_ref,
                r_o, d_o, k_o, v_o, kk_o, b_o, bon_o, g_o):
    p = p_ref[...]
    q = p + (prev_ref[...] - p) * mu_ref[...]
    r = q[:, 0:RW_DIM]
    k = q[:, RW_DIM:2 * RW_DIM]
    v = q[:, 2 * RW_DIM:3 * RW_DIM]
    l2 = q[:, 3 * RW_DIM:3 * RW_DIM + 128]
    lg = q[:, 3 * RW_DIM + 128:3 * RW_DIM + 256]
    ones = ones_ref[...]
    ww = jnp.dot(jnp.tanh(l2), w2_ref[...], precision=HI, preferred_element_type=f32)
    w = -_softplus(-(w0_ref[...] + ww)) - 0.5
    decay = jnp.exp(-jnp.exp(w))
    aa = jnp.dot(l2, a2_ref[...], precision=HI, preferred_element_type=f32)
    a = _sigmoid(a0_ref[...] + aa)
    g = jnp.dot(_sigmoid(lg), g2_ref[...], precision=HI, preferred_element_type=f32)
    kk = k * kk_ref_[...]
    ss = jnp.dot(kk * kk, ones, precision=HI, preferred_element_type=f32)
    kk = kk / jnp.maximum(jnp.sqrt(ss), 1e-12)
    k2 = k * (1.0 + (a - 1.0) * ka_ref[...])
    rk = jnp.dot(r * k2 * rk_ref[...], ones, precision=HI, preferred_element_type=f32)
    r_o[...] = r
    d_o[...] = decay
    k_o[...] = k2
    v_o[...] = v
    kk_o[...] = kk
    b_o[...] = kk * a
    bon_o[...] = rk * v
    g_o[...] = g


def _rwpre(p, prev, prm):
    N = p.shape[0]
    tm = math.gcd(N, 256)
    row = lambda c: pl.BlockSpec((tm, c), lambda i: (i, 0))
    full = lambda a: pl.BlockSpec(a.shape, lambda i: (0, 0))
    params = [prm["mu"], prm["w0"], prm["w2p"], prm["a0"], prm["a2p"], prm["g2"], prm["k_k"], prm["k_a"], prm["r_k"],
              prm["ones64"]]
    return pl.pallas_call(
        _rwpre_body,
        grid=(N // tm,),
        in_specs=[row(RW_COLS), row(RW_COLS)] + [full(a) for a in params],
        out_specs=[row(RW_DIM)] * 8,
        out_shape=[jax.ShapeDtypeStruct((N, RW_DIM), f32)] * 8,
        compiler_params=_cparams(("arbitrary",)),
        name="rwkv_pre",
    )(p, prev, *params)


def _segsum(x, ones):
    hi = x.astype(bf16)
    lo = (x - hi.astype(f32)).astype(bf16)
    return (jnp.dot(hi, ones, preferred_element_type=f32) + jnp.dot(lo, ones, preferred_element_type=f32))


def _scan_body(r_ref, d_ref, k_ref, v_ref, kk_ref, b_ref, s0_ref, y_ref, sT_ref, S, pm_s, p2_s, *, bb, Tc, ntc):
    tci = pl.program_id(1)
    npair = RW_HEADS // 2
    nblk = bb * npair
    R = nblk * RW_HEAD

    @pl.when(tci == 0)
    def _():
        for blk in range(nblk):
            b, p = divmod(blk, npair)
            S[blk * 64:(blk + 1) * 64, :] = jnp.concatenate([s0_ref[b, 2 * p], s0_ref[b, 2 * p + 1]], axis=-1)

    ri = lax.broadcasted_iota(jnp.int32, (LANES, LANES), 0)
    ci = lax.broadcasted_iota(jnp.int32, (LANES, LANES), 1)
    ones = jnp.where((ri // 64) == (ci // 64), 1.0, 0.0).astype(bf16)
    r64 = lax.broadcasted_iota(jnp.int32, (RW_HEAD, LANES), 0)
    c64 = lax.broadcasted_iota(jnp.int32, (RW_HEAD, LANES), 1)
    eye2 = jnp.where((c64 % 64) == r64, 1.0, 0.0).astype(f32)

    def step(t, carry):
        row = lambda ref: [ref[b, pl.ds(t, 1), :] for b in range(bb)]
        kk_t, v_t, d_t, b_t, k_t, r_t = row(kk_ref), row(v_ref), row(d_ref), row(b_ref), row(k_ref), row(r_ref)
        for blk in range(nblk):
            b, p = divmod(blk, npair)
            rows = slice(blk * 64, (blk + 1) * 64)
            ln = slice(p * LANES, (p + 1) * LANES)
            pm_s[rows, :] = S[rows, :] * kk_t[b][:, ln]
            pm_s[R + blk * 64:R + (blk + 1) * 64, :] = eye2 * v_t[b][:, ln]
        sv = _segsum(pm_s[...], ones)
        for blk in range(nblk):
            b, p = divmod(blk, npair)
            rows = slice(blk * 64, (blk + 1) * 64)
            ln = slice(p * LANES, (p + 1) * LANES)
            sn = (S[rows, :] * d_t[b][:, ln]
                  - sv[blk * 64:(blk + 1) * 64, :] * b_t[b][:, ln]
                  + sv[R + blk * 64:R + (blk + 1) * 64, :] * k_t[b][:, ln])
            S[rows, :] = sn
            p2_s[rows, :] = sn * r_t[b][:, ln]
        yb = _segsum(p2_s[...], ones)
        for b in range(bb):
            yrow = [jnp.sum(eye2 * yb[(b * npair + p) * 64:(b * npair + p + 1) * 64, :], axis=0, keepdims=True)
                    for p in range(npair)]
            y_ref[b, pl.ds(t, 1), :] = jnp.concatenate(yrow, axis=-1)
        return carry

    lax.fori_loop(0, Tc, step, 0)

    @pl.when(tci == ntc - 1)
    def _():
        for blk in range(nblk):
            b, p = divmod(blk, npair)
            s = S[blk * 64:(blk + 1) * 64, :]
            sT_ref[b, 2 * p] = s[:, :64]
            sT_ref[b, 2 * p + 1] = s[:, 64:]


def _scan(r, d, k, v, kk, b, s0):
    B, T, _ = r.shape
    bb = min(SCAN_BB, B)
    Tc = min(64, T)
    ntc = T // Tc
    R = bb * (RW_HEADS // 2) * RW_HEAD
    tok = pl.BlockSpec((bb, Tc, RW_DIM), lambda i, j: (i, j, 0))
    st = pl.BlockSpec((bb, RW_HEADS, RW_HEAD, RW_HEAD), lambda i, j: (i, 0, 0, 0))
    return pl.pallas_call(
        functools.partial(_scan_body, bb=bb, Tc=Tc, ntc=ntc),
        grid=(B // bb, ntc),
        in_specs=[tok] * 6 + [st],
        out_specs=[tok, st],
        out_shape=[jax.ShapeDtypeStruct((B, T, RW_DIM), f32),
                   jax.ShapeDtypeStruct((B, RW_HEADS, RW_HEAD, RW_HEAD), f32)],
        scratch_shapes=[pltpu.VMEM((R, LANES), f32), pltpu.VMEM((2 * R, LANES), f32), pltpu.VMEM((R, LANES), f32)],
        compiler_params=_cparams(("arbitrary", "arbitrary")),
        name="rwkv_scan",
    )(r, d, k, v, kk, b, s0)


def _mamba_body(z_ref, xbc_ref, dt_ref, dtT_ref, conv0_ref, ssm0_ref, cw_ref, cb_ref, dtb_ref, dtbT_ref, alog_ref,
                alogT_ref, dexp_ref, nw_ref, ob_ref, convT_ref, ssmT_ref, ext, hst, ybuf, *, L, nc):
    c = pl.program_id(1)
    CD = SSM_CONV_DIM

    @pl.when(c == 0)
    def _():
        ext[0:8, :] = jnp.zeros((8, CD), f32)
        ext[5:8, :] = conv0_ref[0]
        hst[...] = ssm0_ref[0]

    ext[8:8 + L, :] = xbc_ref[...]
    cw = cw_ref[...]
    conv = (cb_ref[...] + ext[5:5 + L, :] * cw[0:1] + ext[6:6 + L, :] * cw[1:2]
            + ext[7:7 + L, :] * cw[2:3] + ext[8:8 + L, :] * cw[3:4])
    xc = conv * _sigmoid(conv)

    @pl.when(c == nc - 1)
    def _():
        convT_ref[0] = ext[L + 5:L + 8, :]

    ext[0:8, :] = ext[L:L + 8, :]

    dtc = _softplus(dt_ref[...] + dtb_ref[...])
    a_col = dtc * (-jnp.exp(alog_ref[...]))
    dtr = _softplus(dtT_ref[0] + dtbT_ref[...])
    a_row = dtr * (-jnp.exp(alogT_ref[...]))
    ri = lax.broadcasted_iota(jnp.int32, (L, L), 0)
    ci = lax.broadcasted_iota(jnp.int32, (L, L), 1)
    lower = ri >= ci
    cs_col = jnp.dot(jnp.where(lower, 1.0, 0.0).astype(f32), a_col, precision=HI, preferred_element_type=f32)
    cs_row = jnp.dot(a_row, jnp.where(ri <= ci, 1.0, 0.0).astype(f32), precision=HI, preferred_element_type=f32)

    xs = xc[:, :SSM_DIM]
    gn = SSM_GROUPS * SSM_STATE
    hpg = SSM_HEADS // SSM_GROUPS
    for g in range(SSM_GROUPS):
        Bb = xc[:, SSM_DIM + g * SSM_STATE:SSM_DIM + (g + 1) * SSM_STATE].astype(bf16)
        Cb = xc[:, SSM_DIM + gn + g * SSM_STATE:SSM_DIM + gn + (g + 1) * SSM_STATE].astype(bf16)
        cbm = _nt(Cb, Bb)
        for j in range(hpg):
            h = g * hpg + j
            col = cs_col[:, h:h + 1]
            row = cs_row[h:h + 1, :]
            ldec = jnp.exp(jnp.where(lower, col - row, NEG))
            gm = cbm * ldec * dtr[h:h + 1, :]
            xh = xs[:, h * SSM_HEAD:(h + 1) * SSM_HEAD]
            yd = jnp.dot(gm.astype(bf16), xh.astype(bf16), preferred_element_type=f32)
            hp = hst[h]
            yo = jnp.exp(col) * _nt(Cb, hp.astype(bf16))
            last = row[:, L - 1:L]
            wcol = jnp.exp(last - col) * dtc[:, h:h + 1]
            hst[h] = jnp.exp(last) * hp + _tn((xh * wcol).astype(bf16), Bb)
            ybuf[:, h * SSM_HEAD:(h + 1) * SSM_HEAD] = yd + yo

    z = z_ref[...]
    y = (ybuf[...] + xs * dexp_ref[...]) * (z * _sigmoid(z))
    gw = SSM_DIM // SSM_GROUPS
    for g in range(SSM_GROUPS):
        yg = y[:, g * gw:(g + 1) * gw]
        ms = jnp.mean(yg * yg, axis=-1, keepdims=True)
        ob_ref[:, g * gw:(g + 1) * gw] = yg * lax.rsqrt(ms + NORM_EPS) * nw_ref[:, g * gw:(g + 1) * gw]

    @pl.when(c == nc - 1)
    def _():
        ssmT_ref[0] = hst[...]


def _mamba(z, xbc, dt, dtT, conv0, ssm0, prm, B, T):
    L =math.gcd(T, SSM_CHUNK)
    nc = T // L
    row = lambda cdim: pl.BlockSpec((L, cdim), lambda b, c: (b * nc + c, 0))
    full = lambda a: pl.BlockSpec(a.shape, lambda b, c: (0,) * a.ndim)
    params = [prm["conv_w"], prm["conv_b"], prm["dt_bias"], prm["dt_biasT"], prm["A_log"], prm["A_logT"],
              prm["D_exp"], prm["norm_w"]]
    return pl.pallas_call(
        functools.partial(_mamba_body, L=L, nc=nc),
        grid=(B, nc),
        in_specs=[row(SSM_DIM), row(SSM_CONV_DIM), row(LANES),
                  pl.BlockSpec((1, SSM_HEADS, L), lambda b, c: (b, 0, c)),
                  pl.BlockSpec((1, SSM_CONV - 1, SSM_CONV_DIM), lambda b, c: (b, 0, 0)),
                  pl.BlockSpec((1, SSM_HEADS, SSM_HEAD, SSM_STATE), lambda b, c: (b, 0, 0, 0))]
                 + [full(a) for a in params],
        out_specs=[row(SSM_DIM),
                   pl.BlockSpec((1, SSM_CONV - 1, SSM_CONV_DIM), lambda b, c: (b, 0, 0)),
                   pl.BlockSpec((1, SSM_HEADS, SSM_HEAD, SSM_STATE), lambda b, c: (b, 0, 0, 0))],
        out_shape=[jax.ShapeDtypeStruct((B * T, SSM_DIM), f32),
                   jax.ShapeDtypeStruct((B, SSM_CONV - 1, SSM_CONV_DIM), f32),
                   jax.ShapeDtypeStruct((B, SSM_HEADS, SSM_HEAD, SSM_STATE), f32)],
        scratch_shapes=[pltpu.VMEM((L + 8, SSM_CONV_DIM), f32),
                        pltpu.VMEM((SSM_HEADS, SSM_HEAD, SSM_STATE), f32),
                        pltpu.VMEM((L, SSM_DIM), f32)],
        compiler_params=_cparams(("arbitrary", "arbitrary")),
        name="mamba",
    )(z, xbc, dt, dtT, conv0, ssm0, *params)


def _merge_body(y_ref, bon_ref, g_ref, ob_ref, gate_ref, x_ref, gt1_ref, sc2_ref, sh2_ref, lnw_ref, lnb_ref, ones_ref,
                wpa_ref, wpb_ref, wout_ref, n2g_ref, x1_ref, xn2_ref):
    y = y_ref[...]
    ones = ones_ref[...]
    inv = 1.0 / RW_HEAD
    mean = jnp.dot(y, ones, precision=HI, preferred_element_type=f32) * inv
    yc = y - mean
    var = jnp.dot(yc * yc, ones, precision=HI, preferred_element_type=f32) * inv
    yn = yc * lax.rsqrt(var + RW_LN_EPS) * lnw_ref[...] + lnb_ref[...]
    oa = (yn + bon_ref[...]) * g_ref[...]
    gate = _sigmoid(gate_ref[...])
    ma = jnp.dot(oa.astype(bf16), wpa_ref[...], preferred_element_type=f32)
    mb = jnp.dot(ob_ref[...].astype(bf16), wpb_ref[...], preferred_element_type=f32)
    merged = gate[:, :D_MODEL] * ma + gate[:, D_MODEL:] * mb
    mo = jnp.dot(merged.astype(bf16), wout_ref[...], preferred_element_type=f32)
    x = x_ref[...]
    bb, tt, Dm = x.shape
    x1 = x + gt1_ref[...] * mo.reshape(bb, tt, Dm)
    x1_ref[...] = x1.reshape(bb * tt, Dm)
    ms = jnp.mean(x1 * x1, axis=-1, keepdims=True)
    xn2 = x1 * lax.rsqrt(ms + NORM_EPS) * n2g_ref[...]
    xn2 = xn2 * (1.0 + sc2_ref[...]) + sh2_ref[...]
    xn2_ref[...] = xn2.reshape(bb * tt, Dm).astype(bf16)


def _merge(y, bon, g, ob, gate, x3, gt1, sc2, sh2, prm):
    B, T, Dm = x3.shape
    bb, tt = _tile_bt(B, T)
    if tt == 512:
        tt = 256
    nt = T // tt
    rows = bb * tt
    row = lambda c: pl.BlockSpec((rows, c), lambda i, j: (i * nt + j, 0))
    mod = pl.BlockSpec((bb, 1, Dm), lambda i, j: (i, 0, 0))
    full = lambda a: pl.BlockSpec(a.shape, lambda i, j: (0,) * a.ndim)
    params = [prm["ln_w"], prm["ln_b"], prm["ones64"], prm["w_pa"], prm["w_pb"], prm["w_out"], prm["norm2_g"]]
    return pl.pallas_call(
        _merge_body,
        grid=(B // bb, nt),
        in_specs=[row(RW_DIM), row(RW_DIM), row(RW_DIM), row(SSM_DIM), row(2 * D_MODEL),
                  pl.BlockSpec((bb, tt, Dm), lambda i, j: (i, j, 0)), mod, mod, mod] + [full(a) for a in params],
        out_specs=[row(Dm), row(Dm)],
        out_shape=[jax.ShapeDtypeStruct((B * T, Dm), f32), jax.ShapeDtypeStruct((B * T, Dm), bf16)],
        compiler_params=_cparams(("arbitrary", "arbitrary")),
        name="merge",
    )(y, bon, g, ob, gate, x3, gt1, sc2, sh2, *params)


def _peer_body(xb_ref, wqT_ref, keys_ref, u_ref, vT_ref, o_ref, s1_s, s2_s, a_s, b_s, tau_s, t_s, cand_s, acc_s, w_s,
               *, ne):
    e = pl.program_id(1)
    xb = xb_ref[...]
    TM = xb.shape[0]
    K = PEER_TOPK
    ncand = sum(K // (a + 1) for a in range(K))

    @pl.when(e == 0)
    def _():
        qT = _nt(wqT_ref[...], xb)
        cand_s[...] = jnp.full(cand_s.shape, NEG, f32)
        for h in range(PEER_HEADS):
            for c in range(2):
                lo = h * PEER_DKEY + c * (PEER_DKEY // 2)
                qh = qT[lo:lo + PEER_DKEY // 2, :].astype(bf16)
                s = jnp.dot(keys_ref[h, c], qh, preferred_element_type=f32)
                if c == 0:
                    s1_s[h] = s
                else:
                    s2_s[h] = s
                work = s
                for it in range(K):
                    m = jnp.max(work, axis=0, keepdims=True)
                    t_s[c * K + it:c * K + it + 1, :] = m
                    work = jnp.where(work == m, NEG, work)
            idx = 0
            for a in range(K):
                nb = K // (a + 1)
                cand_s[idx:idx + nb, :] = t_s[a:a + 1, :] + t_s[K:K + nb, :]
                idx += nb
            cand = cand_s[...]
            rank = jnp.zeros(cand.shape, f32)
            for j in range(ncand):
                rank = rank + jnp.where(cand_s[j:j + 1, :] > cand, 1.0, 0.0)
            tau = jnp.min(jnp.where(rank <= K - 1.0, cand, -NEG), axis=0, keepdims=True)
            mx = cand_s[0:1, :]
            zsum = jnp.sum(jnp.where(cand >= tau, jnp.exp(cand - mx), 0.0), axis=0, keepdims=True)
            a_s[h] = jnp.exp(s1_s[h] - t_s[0:1, :])
            b_s[h] = jnp.exp(s2_s[h] - t_s[K:K + 1, :]) / zsum
            tau_s[h:h + 1, :] = tau
        acc_s[...] = jnp.zeros(acc_s.shape, f32)

    at = _nt(u_ref[...], xb)
    nsub = PEER_EB // PEER_NKEYS
    for ii in range(nsub):
        i = e * nsub + ii
        gsum = jnp.zeros((PEER_NKEYS, TM), f32)
        for h in range(PEER_HEADS):
            csum = s1_s[h, pl.ds(i, 1), :] + s2_s[h]
            gsum = gsum + jnp.where(csum >= tau_s[h:h + 1, :], b_s[h], 0.0) * a_s[h, pl.ds(i, 1), :]
        av = at[ii * PEER_NKEYS:(ii + 1) * PEER_NKEYS, :]
        act = 0.5 * av * (1.0 + lax.erf(av * INV_SQRT2))
        w_s[ii * PEER_NKEYS:(ii + 1) * PEER_NKEYS, :] = (gsum * act).astype(bf16)
    acc_s[...] += jnp.dot(vT_ref[...], w_s[...], preferred_element_type=f32)

    @pl.when(e == ne - 1)
    def _():
        o_ref[...] = acc_s[...].T


def _peer(xb, prm):
    N, Dm = xb.shape
    TM = PEER_TM
    ne = (PEER_NKEYS * PEER_NKEYS) // PEER_EB
    K = PEER_TOPK
    return pl.pallas_call(
        functools.partial(_peer_body, ne=ne),
        grid=(N // TM, ne),
        in_specs=[pl.BlockSpec((TM, Dm), lambda i, e: (i, 0)),
                  pl.BlockSpec(prm["wqT"].shape, lambda i, e: (0, 0)),
                  pl.BlockSpec(prm["keys"].shape, lambda i, e: (0, 0, 0, 0)),
                  pl.BlockSpec((PEER_EB, Dm), lambda i, e: (e, 0)),
                  pl.BlockSpec((Dm, PEER_EB), lambda i, e: (0, e))],
        out_specs=pl.BlockSpec((TM, Dm), lambda i, e: (i, 0)),
        out_shape=jax.ShapeDtypeStruct((N, Dm), f32),
        scratch_shapes=[pltpu.VMEM((PEER_HEADS, PEER_NKEYS, TM), f32)] * 4
                       + [pltpu.VMEM((PEER_HEADS, TM), f32), pltpu.VMEM((2 * K, TM), f32),
                          pltpu.VMEM((56, TM), f32), pltpu.VMEM((Dm, TM), f32), pltpu.VMEM((PEER_EB, TM), bf16)],
        compiler_params=_cparams(("arbitrary", "arbitrary")),
        name="peer",
    )(xb, prm["wqT"], prm["keys"], prm["u"], prm["vT"])


def _final_body(x1_ref, pe_ref, gt2_ref, scf_ref, shf_ref, fg_ref, o_ref):
    bb, tt, Dm = o_ref.shape
    x2 = x1_ref[...].reshape(bb, tt, Dm) + gt2_ref[...] * pe_ref[...].reshape(bb, tt, Dm)
    ms = jnp.mean(x2 * x2, axis=-1, keepdims=True)
    y = x2 * lax.rsqrt(ms + NORM_EPS) * fg_ref[...]
    o_ref[...] = y * (1.0 + scf_ref[...]) + shf_ref[...]


def _final(x1, pe, gt2, scf, shf, fg, B, T):
    Dm = x1.shape[1]
    bb, tt = _tile_bt(B, T)
    nt = T // tt
    row = pl.BlockSpec((bb * tt, Dm), lambda i, j: (i * nt + j, 0))
    mod = pl.BlockSpec((bb, 1, Dm), lambda i, j: (i, 0, 0))
    return pl.pallas_call(
        _final_body,
        grid=(B // bb, nt),
        in_specs=[row, row, mod, mod, mod, pl.BlockSpec((1, 1, Dm), lambda i, j: (0, 0, 0))],
        out_specs=pl.BlockSpec((bb, tt, Dm), lambda i, j: (i, j, 0)),
        out_shape=jax.ShapeDtypeStruct((B, T, Dm), f32),
        compiler_params=_cparams(("arbitrary", "arbitrary")),
        name="final_norm",
    )(x1, pe, gt2, scf, shf, fg)


def _prepare(w_in, rw_mu, rw_w0, rw_w2, rw_a0, rw_a2, rw_g2, rw_k_k, rw_k_a, rw_r_k, rw_ln_w, rw_ln_b, ssm_conv_w,
             ssm_conv_b, ssm_dt_bias, ssm_A_log, ssm_D, ssm_norm_w, w_pa, w_pb, w_out, norm1_g, norm2_g, peer_wq,
             peer_keys, peer_u, peer_v, l):
    w = w_in[l]
    o1 = RW_COLS
    o2 = o1 + SSM_DIM
    o3 = o2 + SSM_CONV_DIM
    o4 = o3 + SSM_HEADS
    seg = lax.broadcasted_iota(jnp.int32, (RW_DIM, RW_DIM), 0) // RW_HEAD
    seg_c = lax.broadcasted_iota(jnp.int32, (RW_DIM, RW_DIM), 1) // RW_HEAD
    zlora = jnp.zeros((64, RW_DIM), f32)
    pad16 = lambda v: jnp.pad(v.reshape(1, SSM_HEADS), ((0, 0), (0, LANES - SSM_HEADS)))
    return dict(
        w_rw=w[:, :o1].astype(bf16), w_z=w[:, o1:o2].astype(bf16), w_xbc=w[:, o2:o3].astype(bf16),
        w_dt=jnp.pad(w[:, o3:o4], ((0, 0), (0, LANES - SSM_HEADS))).astype(bf16), w_gate=w[:, o4:].astype(bf16),
        norm1_g=norm1_g[l].reshape(1, 1, D_MODEL), norm1_g2=norm1_g[l].reshape(1, D_MODEL),
        mu=rw_mu[l].reshape(1, RW_COLS), w0=rw_w0[l].reshape(1, RW_DIM), a0=rw_a0[l].reshape(1, RW_DIM),
        w2p=jnp.concatenate([rw_w2[l], zlora], axis=0), a2p=jnp.concatenate([zlora, rw_a2[l]], axis=0), g2=rw_g2[l],
        k_k=rw_k_k[l].reshape(1, RW_DIM), k_a=rw_k_a[l].reshape(1, RW_DIM), r_k=rw_r_k[l].reshape(1, RW_DIM),
        ln_w=rw_ln_w[l].reshape(1, RW_DIM), ln_b=rw_ln_b[l].reshape(1, RW_DIM),
        ones64=jnp.where(seg == seg_c, 1.0, 0.0).astype(f32),
        conv_w=ssm_conv_w[l], conv_b=ssm_conv_b[l].reshape(1, SSM_CONV_DIM),
        dt_bias=pad16(ssm_dt_bias[l]), dt_biasT=ssm_dt_bias[l].reshape(SSM_HEADS, 1),
        A_log=pad16(ssm_A_log[l]), A_logT=ssm_A_log[l].reshape(SSM_HEADS, 1),
        D_exp=jnp.repeat(ssm_D[l], SSM_HEAD).reshape(1, SSM_DIM), norm_w=ssm_norm_w[l].reshape(1, SSM_DIM),
        w_pa=w_pa[l].astype(bf16), w_pb=w_pb[l].astype(bf16), w_out=w_out[l].astype(bf16),
        norm2_g=norm2_g[l].reshape(1, 1, D_MODEL),
        wqT=peer_wq[l].T.astype(bf16), keys=peer_keys[l].astype(bf16), u=peer_u[l].astype(bf16),
        vT=peer_v[l].T.astype(bf16),
    )


def _layer_pre(x, mod, shift0, wkv0, conv0, ssm0, prm):
    B, T, Dm = x.shape
    N = B * T
    m3 = lambda i: mod[:, i * Dm:(i + 1) * Dm].reshape(B, 1, Dm)
    sh1, sc1, gt1, sh2, sc2, gt2 = [m3(i) for i in range(6)]
    p_rw, dt = _nm_matmul(x, sc1, sh1, prm["norm1_g"], [prm["w_rw"], prm["w_dt"]], "in_proj_rw")
    z, xbc = _nm_matmul(x, sc1, sh1, prm["norm1_g"], [prm["w_z"], prm["w_xbc"]], "in_proj_ssm")
    (gate,) = _nm_matmul(x, sc1, sh1, prm["norm1_g"], [prm["w_gate"]], "in_proj_gate")
    xn_last = _xnlast(x[:, -1, :], sc1[:, 0], sh1[:, 0], prm["norm1_g2"])
    p_prev = _mm(shift0, prm["w_rw"])
    p3 = p_rw.reshape(B, T, RW_COLS)
    prev = jnp.concatenate([p_prev[:, None, :], p3[:, :-1, :]], axis=1).reshape(N, RW_COLS)
    r, d, k, v, kk, b, bon, g = _rwpre(p_rw, prev, prm)
    t3 = lambda a: a.reshape(B, T, RW_DIM)
    y, wkvT = _scan(t3(r), t3(d), t3(k), t3(v), t3(kk), t3(b), wkv0)
    dtT = jnp.transpose(dt[:, :SSM_HEADS].reshape(B, T, SSM_HEADS), (0, 2, 1))
    ob, convT, ssmT = _mamba(z, xbc, dt, dtT, conv0, ssm0, prm, B, T)
    x1, xn2 = _merge(y.reshape(N, RW_DIM), bon, g, ob, gate, x, gt1, sc2, sh2, prm)
    return x1, xn2, gt2, xn_last, wkvT, convT, ssmT


def _trunk(xs, cs, states, prm, w_ada, b_ada, final_g, w_ada_f, b_ada_f):
    Dm = xs[0].shape[-1]
    nb = [x.shape[0] for x in xs]
    offs = [sum(nb[:i]) for i in range(len(xs) + 1)]
    c_all = jnp.concatenate(cs, axis=0)
    mod = _ada(c_all, w_ada, b_ada.reshape(1, -1))
    fm = _ada(c_all, w_ada_f, b_ada_f.reshape(1, -1))
    pre = [_layer_pre(x, mod[offs[i]:offs[i + 1]], *states[i], prm) for i, x in enumerate(xs)]
    xn2_all = jnp.concatenate([p[1] for p in pre], axis=0)
    n_all = xn2_all.shape[0]
    n_pad = -(-n_all // PEER_TM) * PEER_TM
    if n_pad != n_all:
        xn2_all = jnp.pad(xn2_all, ((0, n_pad - n_all), (0, 0)))
    pe_all = _peer(xn2_all, prm)
    outs = []
    tok = 0
    for i, x in enumerate(xs):
        B, T, _ = x.shape
        x1, _, gt2, xn_last, wkvT, convT, ssmT = pre[i]
        m = fm[offs[i]:offs[i + 1]]
        shf = m[:, :Dm].reshape(B, 1, Dm)
        scf = m[:, Dm:].reshape(B, 1, Dm)
        y = _final(x1, pe_all[tok:tok + B * T], gt2, scf, shf, final_g.reshape(1, 1, Dm), B, T)
        tok += B * T
        outs.append((y, xn_last[None], wkvT[None], convT[None], ssmT[None]))
    return outs


def kernel(x_prompt, x_sample, c_prompt, c_sample, state_shift, state_wkv, state_conv, state_ssm, w_ada, b_ada, norm1_g, w_in, rw_mu, rw_w0, rw_w2, rw_a0, rw_a2, rw_g2, rw_k_k, rw_k_a, rw_r_k, rw_ln_w, rw_ln_b, ssm_conv_w, ssm_conv_b, ssm_dt_bias, ssm_A_log, ssm_D, ssm_norm_w, w_pa, w_pb, w_out, norm2_g, peer_wq, peer_keys, peer_u, peer_v, final_g, w_ada_f, b_ada_f):
    assert w_in.shape[0] == 1, "single-layer trunk"
    Bp = x_prompt.shape[0]
    prm = _prepare(w_in, rw_mu, rw_w0, rw_w2, rw_a0, rw_a2, rw_g2, rw_k_k, rw_k_a, rw_r_k, rw_ln_w, rw_ln_b,
                   ssm_conv_w, ssm_conv_b, ssm_dt_bias, ssm_A_log, ssm_D, ssm_norm_w, w_pa, w_pb, w_out,
                   norm1_g, norm2_g, peer_wq, peer_keys, peer_u, peer_v, 0)
    zeros = (jnp.zeros((Bp,) + state_shift.shape[2:], f32), jnp.zeros((Bp,) + state_wkv.shape[2:], f32),
             jnp.zeros((Bp,) + state_conv.shape[2:], f32), jnp.zeros((Bp,) + state_ssm.shape[2:], f32))
    carried = (state_shift[0], state_wkv[0], state_conv[0], state_ssm[0])
    (yp, *sp), (ys, *ss) = _trunk([x_prompt, x_sample], [c_prompt, c_sample], [zeros, carried], prm,
                                  w_ada[0], b_ada[0], final_g, w_ada_f, b_ada_f)
    return (yp, ys, *sp, *ss)
```

```python
import functools
import math

import jax
import jax.numpy as jnp
from jax import lax
from jax.experimental import pallas as pl
from jax.experimental.pallas import tpu as pltpu

f32 = jnp.float32
bf16 = jnp.bfloat16
HI = lax.Precision.HIGHEST

D_MODEL = 1024
RW_HEAD = 64
RW_HEADS = 8
RW_DIM = RW_HEADS * RW_HEAD
RW_COLS = 3 * RW_DIM + 64 + 64 + 128
RW_LN_EPS = 64e-5
SSM_DIM = 1024
SSM_HEAD = 64
SSM_HEADS = 16
SSM_GROUPS = 2
SSM_STATE = 128
SSM_CONV = 4
SSM_CHUNK = 128
SSM_CONV_DIM = SSM_DIM + 2 * SSM_GROUPS * SSM_STATE
PEER_HEADS = 8
PEER_NKEYS = 128
PEER_TOPK = 16
PEER_DKEY = 256
NORM_EPS = 1e-6

LANES = 128
VMEM_LIMIT = 56 * 1024 * 1024
NEG = -1e30
INV_SQRT2 = 0.7071067811865476

PEER_TM = 512
PEER_EB = 1024
SCAN_BB = 8


def _cparams(sem):
    return pltpu.CompilerParams(dimension_semantics=sem, vmem_limit_bytes=VMEM_LIMIT)


def _softplus(x):
    return jnp.maximum(x, 0.0) + jnp.log1p(jnp.exp(-jnp.abs(x)))


def _sigmoid(x):
    return jax.nn.sigmoid(x)


def _nt(a, b):
    return lax.dot_general(a, b, (((1,), (1,)), ((), ())), preferred_element_type=f32)


def _tn(a, b):
    return lax.dot_general(a, b, (((0,), (0,)), ((), ())), preferred_element_type=f32)


def _tile_bt(B, T):
    if T >= 512:
        return 1, 512
    bb = max(1, min(B, 256 // T))
    return bb, T


def _ada_body(c_ref, w_ref, b_ref, o_ref):
    c = c_ref[...]
    s = c * _sigmoid(c)
    o_ref[...] = jnp.dot(s, w_ref[...], precision=HI, preferred_element_type=f32) + b_ref[...]


def _ada(c, w, b):
    M, Dm = c.shape
    Nc = w.shape[1]
    tn = 1024
    return pl.pallas_call(
        _ada_body,
        grid=(Nc // tn,),
        in_specs=[pl.BlockSpec((M, Dm), lambda j: (0, 0)),
                  pl.BlockSpec((Dm, tn), lambda j: (0, j)),
                  pl.BlockSpec((1, tn), lambda j: (0, j))],
        out_specs=pl.BlockSpec((M, tn), lambda j: (0, j)),
        out_shape=jax.ShapeDtypeStruct((M, Nc), f32),
        compiler_params=_cparams(("arbitrary",)),
        name="ada",
    )(c, w, b)


def _nm_body(x_ref, sc_ref, sh_ref, g_ref, *refs, n_w):
    w_refs, o_refs = refs[:n_w], refs[n_w:]
    x = x_ref[...]
    bb, tt, Dm = x.shape
    ms = jnp.mean(x * x, axis=-1, keepdims=True)
    xn = x * lax.rsqrt(ms + NORM_EPS) * g_ref[...]
    xn = xn * (1.0 + sc_ref[...]) + sh_ref[...]
    xb = xn.reshape(bb * tt, Dm).astype(bf16)
    for w_ref, o_ref in zip(w_refs, o_refs):
        o_ref[...] = jnp.dot(xb, w_ref[...], preferred_element_type=f32)


def _nm_matmul(x3, sc, sh, g, ws, name):
    B, T, Dm = x3.shape
    bb, tt = _tile_bt(B, T)
    nt = T // tt
    rows = bb * tt
    in_specs = [pl.BlockSpec((bb, tt, Dm), lambda i, j: (i, j, 0)),
                pl.BlockSpec((bb, 1, Dm), lambda i, j: (i, 0, 0)),
                pl.BlockSpec((bb, 1, Dm), lambda i, j: (i, 0, 0)),
                pl.BlockSpec((1, 1, Dm), lambda i, j: (0, 0, 0))]
    in_specs += [pl.BlockSpec(w.shape, lambda i, j: (0, 0)) for w in ws]
    out_specs = [pl.BlockSpec((rows, w.shape[1]), lambda i, j: (i * nt + j, 0)) for w in ws]
    out_shape = [jax.ShapeDtypeStruct((B * T, w.shape[1]), f32) for w in ws]
    return pl.pallas_call(
        functools.partial(_nm_body, n_w=len(ws)),
        grid=(B // bb, nt),
        in_specs=in_specs, out_specs=out_specs, out_shape=out_shape,
        compiler_params=_cparams(("arbitrary", "arbitrary")),
        name=name,
    )(x3, sc, sh, g, *ws)


def _xnlast_body(x_ref, sc_ref, sh_ref, g_ref, o_ref):
    x = x_ref[...]
    ms = jnp.mean(x * x, axis=-1, keepdims=True)
    xn = x * lax.rsqrt(ms + NORM_EPS) * g_ref[...]
    o_ref[...] = xn * (1.0 + sc_ref[...]) + sh_ref[...]


def _xnlast(x2, sc, sh, g):
    return pl.pallas_call(_xnlast_body, out_shape=jax.ShapeDtypeStruct(x2.shape, f32), name="xn_last")(x2, sc, sh, g)


def _mm_body(x_ref, w_ref, o_ref):
    o_ref[...] = jnp.dot(x_ref[...].astype(bf16), w_ref[...], preferred_element_type=f32)


def _mm(x, w):
    return pl.pallas_call(_mm_body, out_shape=jax.ShapeDtypeStruct((x.shape[0], w.shape[1]), f32),
                          compiler_params=_cparams(None), name="shift_proj")(x, w)


def _rwpre_body(p_ref, prev_ref, mu_ref, w0_ref, w2_ref, a0_ref, a2_ref, g2_ref, kk_ref_, ka_ref, rk_ref, ones_ref,
                r_o, d_o, k_o, v_o, kk_o, b_o, bon_o, g_o):
    p = p_ref[...]
    q = p + (prev_ref[...] - p) * mu_ref[...]
    r = q[:, 0:RW_DIM]
    k = q[:, RW_DIM:2 * RW_DIM]
    v = q[:, 2 * RW_DIM:3 * RW_DIM]
    l2 = q[:, 3 * RW_DIM:3 * RW_DIM + 128]
    lg = q[:, 3 * RW_DIM + 128:3 * RW_DIM + 256]
    ones = ones_ref[...]
    ww = jnp.dot(jnp.tanh(l2), w2_ref[...], precision=HI, preferred_element_type=f32)
    w = -_softplus(-(w0_ref[...] + ww)) - 0.5
    decay = jnp.exp(-jnp.exp(w))
    aa = jnp.dot(l2, a2_ref[...], precision=HI, preferred_element_type=f32)
    a = _sigmoid(a0_ref[...] + aa)
    g = jnp.dot(_sigmoid(lg), g2_ref[...], precision=HI, preferred_element_type=f32)
    kk = k * kk_ref_[...]
    ss = jnp.dot(kk * kk, ones, precision=HI, preferred_element_type=f32)
    kk = kk / jnp.maximum(jnp.sqrt(ss), 1e-12)
    k2 = k * (1.0 + (a - 1.0) * ka_ref[...])
    rk = jnp.dot(r * k2 * rk_ref[...], ones, precision=HI, preferred_element_type=f32)
    r_o[...] = r
    d_o[...] = decay
    k_o[...] = k2
    v_o[...] = v
    kk_o[...] = kk
    b_o[...] = kk * a
    bon_o[...] = rk * v
    g_o[...] = g


def _rwpre(p, prev, prm):
    N = p.shape[0]
    tm = math.gcd(N, 256)
    row = lambda c: pl.BlockSpec((tm, c), lambda i: (i, 0))
    full = lambda a: pl.BlockSpec(a.shape, lambda i: (0, 0))
    params = [prm["mu"], prm["w0"], prm["w2p"], prm["a0"], prm["a2p"], prm["g2"], prm["k_k"], prm["k_a"], prm["r_k"],
              prm["ones64"]]
    return pl.pallas_call(
        _rwpre_body,
        grid=(N // tm,),
        in_specs=[row(RW_COLS), row(RW_COLS)] + [full(a) for a in params],
        out_specs=[row(RW_DIM)] * 8,
        out_shape=[jax.ShapeDtypeStruct((N, RW_DIM), f32)] * 8,
        compiler_params=_cparams(("arbitrary",)),
        name="rwkv_pre",
    )(p, prev, *params)


def _segsum(x, ones):
    hi = x.astype(bf16)
    lo = (x - hi.astype(f32)).astype(bf16)
    return (jnp.dot(hi, ones, preferred_element_type=f32) + jnp.dot(lo, ones, preferred_element_type=f32))


def _scan_body(r_ref, d_ref, k_ref, v_ref, kk_ref, b_ref, s0_ref, y_ref, sT_ref, S, pm_s, p2_s, *, bb, Tc, ntc):
    tci = pl.program_id(1)
    npair = RW_HEADS // 2
    nblk = bb * npair
    R = nblk * RW_HEAD

    @pl.when(tci == 0)
    def _():
        for blk in range(nblk):
            b, p = divmod(blk, npair)
            S[blk * 64:(blk + 1) * 64, :] = jnp.concatenate([s0_ref[b, 2 * p], s0_ref[b, 2 * p + 1]], axis=-1)

    ri = lax.broadcasted_iota(jnp.int32, (LANES, LANES), 0)
    ci = lax.broadcasted_iota(jnp.int32, (LANES, LANES), 1)
    ones = jnp.where((ri // 64) == (ci // 64), 1.0, 0.0).astype(bf16)
    r64 = lax.broadcasted_iota(jnp.int32, (RW_HEAD, LANES), 0)
    c64 = lax.broadcasted_iota(jnp.int32, (RW_HEAD, LANES), 1)
    eye2 = jnp.where((c64 % 64) == r64, 1.0, 0.0).astype(f32)

    def step(t, carry):
        row = lambda ref: [ref[b, pl.ds(t, 1), :] for b in range(bb)]
        kk_t, v_t, d_t, b_t, k_t, r_t = row(kk_ref), row(v_ref), row(d_ref), row(b_ref), row(k_ref), row(r_ref)
        for blk in range(nblk):
            b, p = divmod(blk, npair)
            rows = slice(blk * 64, (blk + 1) * 64)
            ln = slice(p * LANES, (p + 1) * LANES)
            pm_s[rows, :] = S[rows, :] * kk_t[b][:, ln]
            pm_s[R + blk * 64:R + (blk + 1) * 64, :] = eye2 * v_t[b][:, ln]
        sv = _segsum(pm_s[...], ones)
        for blk in range(nblk):
            b, p = divmod(blk, npair)
            rows = slice(blk * 64, (blk + 1) * 64)
            ln = slice(p * LANES, (p + 1) * LANES)
            sn = (S[rows, :] * d_t[b][:, ln]
                  - sv[blk * 64:(blk + 1) * 64, :] * b_t[b][:, ln]
                  + sv[R + blk * 64:R + (blk + 1) * 64, :] * k_t[b][:, ln])
            S[rows, :] = sn
            p2_s[rows, :] = sn * r_t[b][:, ln]
        yb = _segsum(p2_s[...], ones)
        for b in range(bb):
            yrow = [jnp.sum(eye2 * yb[(b * npair + p) * 64:(b * npair + p + 1) * 64, :], axis=0, keepdims=True)
                    for p in range(npair)]
            y_ref[b, pl.ds(t, 1), :] = jnp.concatenate(yrow, axis=-1)
        return carry

    lax.fori_loop(0, Tc, step, 0)

    @pl.when(tci == ntc - 1)
    def _():
        for blk in range(nblk):
            b, p = divmod(blk, npair)
            s = S[blk * 64:(blk + 1) * 64, :]
            sT_ref[b, 2 * p] = s[:, :64]
            sT_ref[b, 2 * p + 1] = s[:, 64:]


def _scan(r, d, k, v, kk, b, s0):
    B, T, _ = r.shape
    bb = min(SCAN_BB, B)
    Tc = min(64, T)
    ntc = T // Tc
    R = bb * (RW_HEADS // 2) * RW_HEAD
    tok = pl.BlockSpec((bb, Tc, RW_DIM), lambda i, j: (i, j, 0))
    st = pl.BlockSpec((bb, RW_HEADS, RW_HEAD, RW_HEAD), lambda i, j: (i, 0, 0, 0))
    return pl.pallas_call(
        functools.partial(_scan_body, bb=bb, Tc=Tc, ntc=ntc),
        grid=(B // bb, ntc),
        in_specs=[tok] * 6 + [st],
        out_specs=[tok, st],
        out_shape=[jax.ShapeDtypeStruct((B, T, RW_DIM), f32),
                   jax.ShapeDtypeStruct((B, RW_HEADS, RW_HEAD, RW_HEAD), f32)],
        scratch_shapes=[pltpu.VMEM((R, LANES), f32), pltpu.VMEM((2 * R, LANES), f32), pltpu.VMEM((R, LANES), f32)],
        compiler_params=_cparams(("arbitrary", "arbitrary")),
        name="rwkv_scan",
    )(r, d, k, v, kk, b, s0)


def _mamba_body(z_ref, xbc_ref, dt_ref, dtT_ref, conv0_ref, ssm0_ref, cw_ref, cb_ref, dtb_ref, dtbT_ref, alog_ref,
                alogT_ref, dexp_ref, nw_ref, ob_ref, convT_ref, ssmT_ref, ext, hst, ybuf, *, L, nc):
    c = pl.program_id(1)
    CD = SSM_CONV_DIM

    @pl.when(c == 0)
    def _():
        ext[0:8, :] = jnp.zeros((8, CD), f32)
        ext[5:8, :] = conv0_ref[0]
        hst[...] = ssm0_ref[0]

    ext[8:8 + L, :] = xbc_ref[...]
    cw = cw_ref[...]
    conv = (cb_ref[...] + ext[5:5 + L, :] * cw[0:1] + ext[6:6 + L, :] * cw[1:2]
            + ext[7:7 + L, :] * cw[2:3] + ext[8:8 + L, :] * cw[3:4])
    xc = conv * _sigmoid(conv)

    @pl.when(c == nc - 1)
    def _():
        convT_ref[0] = ext[L + 5:L + 8, :]

    ext[0:8, :] = ext[L:L + 8, :]

    dtc = _softplus(dt_ref[...] + dtb_ref[...])
    a_col = dtc * (-jnp.exp(alog_ref[...]))
    dtr = _softplus(dtT_ref[0] + dtbT_ref[...])
    a_row = dtr * (-jnp.exp(alogT_ref[...]))
    ri = lax.broadcasted_iota(jnp.int32, (L, L), 0)
    ci = lax.broadcasted_iota(jnp.int32, (L, L), 1)
    lower = ri >= ci
    cs_col = jnp.dot(jnp.where(lower, 1.0, 0.0).astype(f32), a_col, precision=HI, preferred_element_type=f32)
    cs_row = jnp.dot(a_row, jnp.where(ri <= ci, 1.0, 0.0).astype(f32), precision=HI, preferred_element_type=f32)

    xs = xc[:, :SSM_DIM]
    gn = SSM_GROUPS * SSM_STATE
    hpg = SSM_HEADS // SSM_GROUPS
    for g in range(SSM_GROUPS):
        Bb = xc[:, SSM_DIM + g * SSM_STATE:SSM_DIM + (g + 1) * SSM_STATE].astype(bf16)
        Cb = xc[:, SSM_DIM + gn + g * SSM_STATE:SSM_DIM + gn + (g + 1) * SSM_STATE].astype(bf16)
        cbm = _nt(Cb, Bb)
        for j in range(hpg):
            h = g * hpg + j
            col = cs_col[:, h:h + 1]
            row = cs_row[h:h + 1, :]
            ldec = jnp.exp(jnp.where(lower, col - row, NEG))
            gm = cbm * ldec * dtr[h:h + 1, :]
            xh = xs[:, h * SSM_HEAD:(h + 1) * SSM_HEAD]
            yd = jnp.dot(gm.astype(bf16), xh.astype(bf16), preferred_element_type=f32)
            hp = hst[h]
            yo = jnp.exp(col) * _nt(Cb, hp.astype(bf16))
            last = row[:, L - 1:L]
            wcol = jnp.exp(last - col) * dtc[:, h:h + 1]
            hst[h] = jnp.exp(last) * hp + _tn((xh * wcol).astype(bf16), Bb)
            ybuf[:, h * SSM_HEAD:(h + 1) * SSM_HEAD] = yd + yo

    z = z_ref[...]
    y = (ybuf[...] + xs * dexp_ref[...]) * (z * _sigmoid(z))
    gw = SSM_DIM // SSM_GROUPS
    for g in range(SSM_GROUPS):
        yg = y[:, g * gw:(g + 1) * gw]
        ms = jnp.mean(yg * yg, axis=-1, keepdims=True)
        ob_ref[:, g * gw:(g + 1) * gw] = yg * lax.rsqrt(ms + NORM_EPS) * nw_ref[:, g * gw:(g + 1) * gw]

    @pl.when(c == nc - 1)
    def _():
        ssmT_ref[0] = hst[...]


def _mamba(z, xbc, dt, dtT, conv0, ssm0, prm, B, T):
    L =math.gcd(T, SSM_CHUNK)
    nc = T // L
    row = lambda cdim: pl.BlockSpec((L, cdim), lambda b, c: (b * nc + c, 0))
    full = lambda a: pl.BlockSpec(a.shape, lambda b, c: (0,) * a.ndim)
    params = [prm["conv_w"], prm["conv_b"], prm["dt_bias"], prm["dt_biasT"], prm["A_log"], prm["A_logT"],
              prm["D_exp"], prm["norm_w"]]
    return pl.pallas_call(
        functools.partial(_mamba_body, L=L, nc=nc),
        grid=(B, nc),
        in_specs=[row(SSM_DIM), row(SSM_CONV_DIM), row(LANES),
                  pl.BlockSpec((1, SSM_HEADS, L), lambda b, c: (b, 0, c)),
                  pl.BlockSpec((1, SSM_CONV - 1, SSM_CONV_DIM), lambda b, c: (b, 0, 0)),
                  pl.BlockSpec((1, SSM_HEADS, SSM_HEAD, SSM_STATE), lambda b, c: (b, 0, 0, 0))]
                 + [full(a) for a in params],
        out_specs=[row(SSM_DIM),
                   pl.BlockSpec((1, SSM_CONV - 1, SSM_CONV_DIM), lambda b, c: (b, 0, 0)),
                   pl.BlockSpec((1, SSM_HEADS, SSM_HEAD, SSM_STATE), lambda b, c: (b, 0, 0, 0))],
        out_shape=[jax.ShapeDtypeStruct((B * T, SSM_DIM), f32),
                   jax.ShapeDtypeStruct((B, SSM_CONV - 1, SSM_CONV_DIM), f32),
                   jax.ShapeDtypeStruct((B, SSM_HEADS, SSM_HEAD, SSM_STATE), f32)],
        scratch_shapes=[pltpu.VMEM((L + 8, SSM_CONV_DIM), f32),
                        pltpu.VMEM((SSM_HEADS, SSM_HEAD, SSM_STATE), f32),
                        pltpu.VMEM((L, SSM_DIM), f32)],
        compiler_params=_cparams(("arbitrary", "arbitrary")),
        name="mamba",
    )(z, xbc, dt, dtT, conv0, ssm0, *params)


def _merge_body(y_ref, bon_ref, g_ref, ob_ref, gate_ref, x_ref, gt1_ref, sc2_ref, sh2_ref, lnw_ref, lnb_ref, ones_ref,
                wpa_ref, wpb_ref, wout_ref, n2g_ref, x1_ref, xn2_ref):
    y = y_ref[...]
    ones = ones_ref[...]
    inv = 1.0 / RW_HEAD
    mean = jnp.dot(y, ones, precision=HI, preferred_element_type=f32) * inv
    yc = y - mean
    var = jnp.dot(yc * yc, ones, precision=HI, preferred_element_type=f32) * inv
    yn = yc * lax.rsqrt(var + RW_LN_EPS) * lnw_ref[...] + lnb_ref[...]
    oa = (yn + bon_ref[...]) * g_ref[...]
    gate = _sigmoid(gate_ref[...])
    ma = jnp.dot(oa.astype(bf16), wpa_ref[...], preferred_element_type=f32)
    mb = jnp.dot(ob_ref[...].astype(bf16), wpb_ref[...], preferred_element_type=f32)
    merged = gate[:, :D_MODEL] * ma + gate[:, D_MODEL:] * mb
    mo = jnp.dot(merged.astype(bf16), wout_ref[...], preferred_element_type=f32)
    x = x_ref[...]
    bb, tt, Dm = x.shape
    x1 = x + gt1_ref[...] * mo.reshape(bb, tt, Dm)
    x1_ref[...] = x1.reshape(bb * tt, Dm)
    ms = jnp.mean(x1 * x1, axis=-1, keepdims=True)
    xn2 = x1 * lax.rsqrt(ms + NORM_EPS) * n2g_ref[...]
    xn2 = xn2 * (1.0 + sc2_ref[...]) + sh2_ref[...]
    xn2_ref[...] = xn2.reshape(bb * tt, Dm).astype(bf16)


def _merge(y, bon, g, ob, gate, x3, gt1, sc2, sh2, prm):
    B, T, Dm = x3.shape
    bb, tt = _tile_bt(B, T)
    if tt == 512:
        tt = 256
    nt = T // tt
    rows = bb * tt
    row = lambda c: pl.BlockSpec((rows, c), lambda i, j: (i * nt + j, 0))
    mod = pl.BlockSpec((bb, 1, Dm), lambda i, j: (i, 0, 0))
    full = lambda a: pl.BlockSpec(a.shape, lambda i, j: (0,) * a.ndim)
    params = [prm["ln_w"], prm["ln_b"], prm["ones64"], prm["w_pa"], prm["w_pb"], prm["w_out"], prm["norm2_g"]]
    return pl.pallas_call(
        _merge_body,
        grid=(B // bb, nt),
        in_specs=[row(RW_DIM), row(RW_DIM), row(RW_DIM), row(SSM_DIM), row(2 * D_MODEL),
                  pl.BlockSpec((bb, tt, Dm), lambda i, j: (i, j, 0)), mod, mod, mod] + [full(a) for a in params],
        out_specs=[row(Dm), row(Dm)],
        out_shape=[jax.ShapeDtypeStruct((B * T, Dm), f32), jax.ShapeDtypeStruct((B * T, Dm), bf16)],
        compiler_params=_cparams(("arbitrary", "arbitrary")),
        name="merge",
    )(y, bon, g, ob, gate, x3, gt1, sc2, sh2, *params)


def _peer_body(xb_ref, wqT_ref, keys_ref, u_ref, vT_ref, o_ref, qT_s, sc_s, cnt_s, a_s, r2_s, b_s, t_s, cand_s, acc_s,
               w_s, *, ne):
    e = pl.program_id(1)
    xb = xb_ref[...]
    TM = xb.shape[0]
    K = PEER_TOPK
    NK = PEER_NKEYS
    half = PEER_DKEY // 2
    nlt = TM // LANES
    ncand = sum(K // (a + 1) for a in range(K))

    @pl.when(e == 0)
    def _():
        qT_s[...] = _nt(wqT_ref[...], xb)
        cand_s[...] = jnp.full(cand_s.shape, NEG, f32)

        def head(h, carry):
            for c in range(2):
                lo = pl.multiple_of(h * PEER_DKEY + c * half, half)
                qh = qT_s[pl.ds(lo, half), :].astype(bf16)
                sc_s[c] = jnp.dot(keys_ref[h, c], qh, preferred_element_type=f32)
            for lt in range(nlt):
                ln = slice(lt * LANES, (lt + 1) * LANES)
                s1 = sc_s[0, :, ln]
                s2 = sc_s[1, :, ln]
                work = s1
                for it in range(K):
                    m = jnp.max(work, axis=0, keepdims=True)
                    t_s[it:it + 1, ln] = m
                    work = jnp.where(work == m, NEG, work)
                work = s2
                r2 = jnp.full((NK, LANES), float(K), f32)
                for it in range(K):
                    m = jnp.max(work, axis=0, keepdims=True)
                    t_s[K + it:K + it + 1, ln] = m
                    hit = work == m
                    r2 = jnp.where(hit, float(it), r2)
                    work = jnp.where(hit, NEG, work)
                idx = 0
                for a in range(K):
                    nb = K // (a + 1)
                    cand_s[idx:idx + nb, ln] = t_s[a:a + 1, ln] + t_s[K:K + nb, ln]
                    idx += nb
                cand = cand_s[:, ln]
                rank = jnp.zeros(cand.shape, f32)
                for j in range(ncand):
                    rank = rank + jnp.where(cand_s[j:j + 1, ln] > cand, 1.0, 0.0)
                tau = jnp.min(jnp.where(rank <= K - 1.0, cand, -NEG), axis=0, keepdims=True)
                mx = cand_s[0:1, ln]
                zsum = jnp.sum(jnp.where(cand >= tau, jnp.exp(cand - mx), 0.0), axis=0, keepdims=True)
                t1 = t_s[0:K, ln]
                cnt = jnp.zeros((NK, LANES), f32)
                for b in range(K):
                    ok = (t1 + t_s[K + b:K + b + 1, ln]) >= tau
                    theta = jnp.min(jnp.where(ok, t1, -NEG), axis=0, keepdims=True)
                    cnt = cnt + jnp.where(s1 >= theta, 1.0, 0.0)
                cnt_s[h, :, ln] = cnt
                a_s[h, :, ln] = jnp.exp(s1 - t_s[0:1, ln])
                r2_s[h, :, :, ln] = r2.astype(bf16).reshape(NK // 16, 16, LANES)
                bv = jnp.exp(s2 - t_s[K:K + 1, ln]) / zsum
                b_s[h, :, :, ln] = bv.astype(bf16).reshape(NK // 16, 16, LANES)
            return carry

        lax.fori_loop(0, PEER_HEADS, head, 0)
        acc_s[...] = jnp.zeros(acc_s.shape, f32)

    at = _nt(u_ref[...], xb)
    nsub = PEER_EB // NK
    for ii in range(nsub):
        i = e * nsub + ii
        gsum = jnp.zeros((NK // 16, 16, TM), bf16)
        for h in range(PEER_HEADS):
            crow = jnp.broadcast_to(cnt_s[h, pl.ds(i, 1), :], (16, TM)).astype(bf16)[None]
            arow = jnp.broadcast_to(a_s[h, pl.ds(i, 1), :], (16, TM)).astype(bf16)[None]
            gsum = gsum + jnp.where(r2_s[h] < crow, b_s[h], jnp.zeros((), bf16)) * arow
        av = at[ii * NK:(ii + 1) * NK, :]
        act = 0.5 * av * (1.0 + lax.erf(av * INV_SQRT2))
        w_s[ii * NK:(ii + 1) * NK, :] = gsum.reshape(NK, TM) * act.astype(bf16)
    acc_s[...] += jnp.dot(vT_ref[...], w_s[...], preferred_element_type=f32)

    @pl.when(e == ne - 1)
    def _():
        o_ref[...] = acc_s[...].T


def _peer(xb, prm):
    N, Dm = xb.shape
    TM = PEER_TM
    ne = (PEER_NKEYS * PEER_NKEYS) // PEER_EB
    K = PEER_TOPK
    return pl.pallas_call(
        functools.partial(_peer_body, ne=ne),
        grid=(N // TM, ne),
        in_specs=[pl.BlockSpec((TM, Dm), lambda i, e: (i, 0)),
                  pl.BlockSpec(prm["wqT"].shape, lambda i, e: (0, 0)),
                  pl.BlockSpec(prm["keys"].shape, lambda i, e: (0, 0, 0, 0)),
                  pl.BlockSpec((PEER_EB, Dm), lambda i, e: (e, 0)),
                  pl.BlockSpec((Dm, PEER_EB), lambda i, e: (0, e))],
        out_specs=pl.BlockSpec((TM, Dm), lambda i, e: (i, 0)),
        out_shape=jax.ShapeDtypeStruct((N, Dm), f32),
        scratch_shapes=[pltpu.VMEM((PEER_HEADS * PEER_DKEY, TM), f32), pltpu.VMEM((2, PEER_NKEYS, TM), f32),
                        pltpu.VMEM((PEER_HEADS, PEER_NKEYS, TM), f32), pltpu.VMEM((PEER_HEADS, PEER_NKEYS, TM), f32),
                        pltpu.VMEM((PEER_HEADS, PEER_NKEYS // 16, 16, TM), bf16),
                        pltpu.VMEM((PEER_HEADS, PEER_NKEYS // 16, 16, TM), bf16),
                        pltpu.VMEM((2 * K, TM), f32), pltpu.VMEM((56, TM), f32), pltpu.VMEM((Dm, TM), f32),
                        pltpu.VMEM((PEER_EB, TM), bf16)],
        compiler_params=_cparams(("arbitrary", "arbitrary")),
        name="peer",
    )(xb, prm["wqT"], prm["keys"], prm["u"], prm["vT"])


def _final_body(x1_ref, pe_ref, gt2_ref, scf_ref, shf_ref, fg_ref, o_ref):
    bb, tt, Dm = o_ref.shape
    x2 = x1_ref[...].reshape(bb, tt, Dm) + gt2_ref[...] * pe_ref[...].reshape(bb, tt, Dm)
    ms = jnp.mean(x2 * x2, axis=-1, keepdims=True)
    y = x2 * lax.rsqrt(ms + NORM_EPS) * fg_ref[...]
    o_ref[...] = y * (1.0 + scf_ref[...]) + shf_ref[...]


def _final(x1, pe, gt2, scf, shf, fg, B, T):
    Dm = x1.shape[1]
    bb, tt = _tile_bt(B, T)
    nt = T // tt
    row = pl.BlockSpec((bb * tt, Dm), lambda i, j: (i * nt + j, 0))
    mod = pl.BlockSpec((bb, 1, Dm), lambda i, j: (i, 0, 0))
    return pl.pallas_call(
        _final_body,
        grid=(B // bb, nt),
        in_specs=[row, row, mod, mod, mod, pl.BlockSpec((1, 1, Dm), lambda i, j: (0, 0, 0))],
        out_specs=pl.BlockSpec((bb, tt, Dm), lambda i, j: (i, j, 0)),
        out_shape=jax.ShapeDtypeStruct((B, T, Dm), f32),
        compiler_params=_cparams(("arbitrary", "arbitrary")),
        name="final_norm",
    )(x1, pe, gt2, scf, shf, fg)


def _prepare(w_in, rw_mu, rw_w0, rw_w2, rw_a0, rw_a2, rw_g2, rw_k_k, rw_k_a, rw_r_k, rw_ln_w, rw_ln_b, ssm_conv_w,
             ssm_conv_b, ssm_dt_bias, ssm_A_log, ssm_D, ssm_norm_w, w_pa, w_pb, w_out, norm1_g, norm2_g, peer_wq,
             peer_keys, peer_u, peer_v, l):
    w = w_in[l]
    o1 = RW_COLS
    o2 = o1 + SSM_DIM
    o3 = o2 + SSM_CONV_DIM
    o4 = o3 + SSM_HEADS
    seg = lax.broadcasted_iota(jnp.int32, (RW_DIM, RW_DIM), 0) // RW_HEAD
    seg_c = lax.broadcasted_iota(jnp.int32, (RW_DIM, RW_DIM), 1) // RW_HEAD
    zlora = jnp.zeros((64, RW_DIM), f32)
    pad16 = lambda v: jnp.pad(v.reshape(1, SSM_HEADS), ((0, 0), (0, LANES - SSM_HEADS)))
    return dict(
        w_rw=w[:, :o1].astype(bf16), w_z=w[:, o1:o2].astype(bf16), w_xbc=w[:, o2:o3].astype(bf16),
        w_dt=jnp.pad(w[:, o3:o4], ((0, 0), (0, LANES - SSM_HEADS))).astype(bf16), w_gate=w[:, o4:].astype(bf16),
        norm1_g=norm1_g[l].reshape(1, 1, D_MODEL), norm1_g2=norm1_g[l].reshape(1, D_MODEL),
        mu=rw_mu[l].reshape(1, RW_COLS), w0=rw_w0[l].reshape(1, RW_DIM), a0=rw_a0[l].reshape(1, RW_DIM),
        w2p=jnp.concatenate([rw_w2[l], zlora], axis=0), a2p=jnp.concatenate([zlora, rw_a2[l]], axis=0), g2=rw_g2[l],
        k_k=rw_k_k[l].reshape(1, RW_DIM), k_a=rw_k_a[l].reshape(1, RW_DIM), r_k=rw_r_k[l].reshape(1, RW_DIM),
        ln_w=rw_ln_w[l].reshape(1, RW_DIM), ln_b=rw_ln_b[l].reshape(1, RW_DIM),
        ones64=jnp.where(seg == seg_c, 1.0, 0.0).astype(f32),
        conv_w=ssm_conv_w[l], conv_b=ssm_conv_b[l].reshape(1, SSM_CONV_DIM),
        dt_bias=pad16(ssm_dt_bias[l]), dt_biasT=ssm_dt_bias[l].reshape(SSM_HEADS, 1),
        A_log=pad16(ssm_A_log[l]), A_logT=ssm_A_log[l].reshape(SSM_HEADS, 1),
        D_exp=jnp.repeat(ssm_D[l], SSM_HEAD).reshape(1, SSM_DIM), norm_w=ssm_norm_w[l].reshape(1, SSM_DIM),
        w_pa=w_pa[l].astype(bf16), w_pb=w_pb[l].astype(bf16), w_out=w_out[l].astype(bf16),
        norm2_g=norm2_g[l].reshape(1, 1, D_MODEL),
        wqT=peer_wq[l].T.astype(bf16), keys=peer_keys[l].astype(bf16), u=peer_u[l].astype(bf16),
        vT=peer_v[l].T.astype(bf16),
    )


def _layer_pre(x, mod, shift0, wkv0, conv0, ssm0, prm):
    B, T, Dm = x.shape
    N = B * T
    m3 = lambda i: mod[:, i * Dm:(i + 1) * Dm].reshape(B, 1, Dm)
    sh1, sc1, gt1, sh2, sc2, gt2 = [m3(i) for i in range(6)]
    p_rw, dt = _nm_matmul(x, sc1, sh1, prm["norm1_g"], [prm["w_rw"], prm["w_dt"]], "in_proj_rw")
    z, xbc = _nm_matmul(x, sc1, sh1, prm["norm1_g"], [prm["w_z"], prm["w_xbc"]], "in_proj_ssm")
    (gate,) = _nm_matmul(x, sc1, sh1, prm["norm1_g"], [prm["w_gate"]], "in_proj_gate")
    xn_last = _xnlast(x[:, -1, :], sc1[:, 0], sh1[:, 0], prm["norm1_g2"])
    p_prev = _mm(shift0, prm["w_rw"])
    p3 = p_rw.reshape(B, T, RW_COLS)
    prev = jnp.concatenate([p_prev[:, None, :], p3[:, :-1, :]], axis=1).reshape(N, RW_COLS)
    r, d, k, v, kk, b, bon, g = _rwpre(p_rw, prev, prm)
    t3 = lambda a: a.reshape(B, T, RW_DIM)
    y, wkvT = _scan(t3(r), t3(d), t3(k), t3(v), t3(kk), t3(b), wkv0)
    dtT = jnp.transpose(dt[:, :SSM_HEADS].reshape(B, T, SSM_HEADS), (0, 2, 1))
    ob, convT, ssmT = _mamba(z, xbc, dt, dtT, conv0, ssm0, prm, B, T)
    x1, xn2 = _merge(y.reshape(N, RW_DIM), bon, g, ob, gate, x, gt1, sc2, sh2, prm)
    return x1, xn2, gt2, xn_last, wkvT, convT, ssmT


def _trunk(xs, cs, states, prm, w_ada, b_ada, final_g, w_ada_f, b_ada_f):
    Dm = xs[0].shape[-1]
    nb = [x.shape[0] for x in xs]
    offs = [sum(nb[:i]) for i in range(len(xs) + 1)]
    c_all = jnp.concatenate(cs, axis=0)
    mod = _ada(c_all, w_ada, b_ada.reshape(1, -1))
    fm = _ada(c_all, w_ada_f, b_ada_f.reshape(1, -1))
    pre = [_layer_pre(x, mod[offs[i]:offs[i + 1]], *states[i], prm) for i, x in enumerate(xs)]
    xn2_all = jnp.concatenate([p[1] for p in pre], axis=0)
    n_all = xn2_all.shape[0]
    n_pad = -(-n_all // PEER_TM) * PEER_TM
    if n_pad != n_all:
        xn2_all = jnp.pad(xn2_all, ((0, n_pad - n_all), (0, 0)))
    pe_all = _peer(xn2_all, prm)
    outs = []
    tok = 0
    for i, x in enumerate(xs):
        B, T, _ = x.shape
        x1, _, gt2, xn_last, wkvT, convT, ssmT = pre[i]
        m = fm[offs[i]:offs[i + 1]]
        shf = m[:, :Dm].reshape(B, 1, Dm)
        scf = m[:, Dm:].reshape(B, 1, Dm)
        y = _final(x1, pe_all[tok:tok + B * T], gt2, scf, shf, final_g.reshape(1, 1, Dm), B, T)
        tok += B * T
        outs.append((y, xn_last[None], wkvT[None], convT[None], ssmT[None]))
    return outs


def kernel(x_prompt, x_sample, c_prompt, c_sample, state_shift, state_wkv, state_conv, state_ssm, w_ada, b_ada, norm1_g, w_in, rw_mu, rw_w0, rw_w2, rw_a0, rw_a2, rw_g2, rw_k_k, rw_k_a, rw_r_k, rw_ln_w, rw_ln_b, ssm_conv_w, ssm_conv_b, ssm_dt_bias, ssm_A_log, ssm_D, ssm_norm_w, w_pa, w_pb, w_out, norm2_g, peer_wq, peer_keys, peer_u, peer_v, final_g, w_ada_f, b_ada_f):
    assert w_in.shape[0] == 1, "single-layer trunk"
    Bp = x_prompt.shape[0]
    prm = _prepare(w_in, rw_mu, rw_w0, rw_w2, rw_a0, rw_a2, rw_g2, rw_k_k, rw_k_a, rw_r_k, rw_ln_w, rw_ln_b,
                   ssm_conv_w, ssm_conv_b, ssm_dt_bias, ssm_A_log, ssm_D, ssm_norm_w, w_pa, w_pb, w_out,
                   norm1_g, norm2_g, peer_wq, peer_keys, peer_u, peer_v, 0)
    zeros = (jnp.zeros((Bp,) + state_shift.shape[2:], f32), jnp.zeros((Bp,) + state_wkv.shape[2:], f32),
             jnp.zeros((Bp,) + state_conv.shape[2:], f32), jnp.zeros((Bp,) + state_ssm.shape[2:], f32))
    carried = (state_shift[0], state_wkv[0], state_conv[0], state_ssm[0])
    (yp, *sp), (ys, *ss) = _trunk([x_prompt, x_sample], [c_prompt, c_sample], [zeros, carried], prm,
                                  w_ada[0], b_ada[0], final_g, w_ada_f, b_ada_f)
    return (yp, ys, *sp, *ss)
```

```python
import functools
import math

import jax
import jax.numpy as jnp
from jax import lax
from jax.experimental import pallas as pl
from jax.experimental.pallas import tpu as pltpu

f32 = jnp.float32
bf16 = jnp.bfloat16
HI = lax.Precision.HIGHEST

D_MODEL = 1024
RW_HEAD = 64
RW_HEADS = 8
RW_DIM = RW_HEADS * RW_HEAD
RW_COLS = 3 * RW_DIM + 64 + 64 + 128
RW_LN_EPS = 64e-5
SSM_DIM = 1024
SSM_HEAD = 64
SSM_HEADS = 16
SSM_GROUPS = 2
SSM_STATE = 128
SSM_CONV = 4
SSM_CHUNK = 128
SSM_CONV_DIM = SSM_DIM + 2 * SSM_GROUPS * SSM_STATE
PEER_HEADS = 8
PEER_NKEYS = 128
PEER_TOPK = 16
PEER_DKEY = 256
NORM_EPS = 1e-6

LANES = 128
VMEM_LIMIT = 56 * 1024 * 1024
NEG = -1e30
INV_SQRT2 = 0.7071067811865476

PEER_TM = 512
PEER_EB = 1024
SCAN_BB = 8


def _cparams(sem):
    return pltpu.CompilerParams(dimension_semantics=sem, vmem_limit_bytes=VMEM_LIMIT)


def _softplus(x):
    return jnp.maximum(x, 0.0) + jnp.log1p(jnp.exp(-jnp.abs(x)))


def _sigmoid(x):
    return jax.nn.sigmoid(x)


def _nt(a, b):
    return lax.dot_general(a, b, (((1,), (1,)), ((), ())), preferred_element_type=f32)


def _tn(a, b):
    return lax.dot_general(a, b, (((0,), (0,)), ((), ())), preferred_element_type=f32)


def _tile_bt(B, T):
    if T >= 512:
        return 1, 512
    bb = max(1, min(B, 256 // T))
    return bb, T


def _ada_body(c_ref, w_ref, b_ref, o_ref):
    c = c_ref[...]
    s = c * _sigmoid(c)
    o_ref[...] = jnp.dot(s, w_ref[...], precision=HI, preferred_element_type=f32) + b_ref[...]


def _ada(c, w, b):
    M, Dm = c.shape
    Nc = w.shape[1]
    tn = 1024
    return pl.pallas_call(
        _ada_body,
        grid=(Nc // tn,),
        in_specs=[pl.BlockSpec((M, Dm), lambda j: (0, 0)),
                  pl.BlockSpec((Dm, tn), lambda j: (0, j)),
                  pl.BlockSpec((1, tn), lambda j: (0, j))],
        out_specs=pl.BlockSpec((M, tn), lambda j: (0, j)),
        out_shape=jax.ShapeDtypeStruct((M, Nc), f32),
        compiler_params=_cparams(("arbitrary",)),
        name="ada",
    )(c, w, b)


def _nm_body(x_ref, sc_ref, sh_ref, g_ref, *refs, n_w):
    w_refs, o_refs = refs[:n_w], refs[n_w:]
    x = x_ref[...]
    bb, tt, Dm = x.shape
    ms = jnp.mean(x * x, axis=-1, keepdims=True)
    xn = x * lax.rsqrt(ms + NORM_EPS) * g_ref[...]
    xn = xn * (1.0 + sc_ref[...]) + sh_ref[...]
    xb = xn.reshape(bb * tt, Dm).astype(bf16)
    for w_ref, o_ref in zip(w_refs, o_refs):
        o_ref[...] = jnp.dot(xb, w_ref[...], preferred_element_type=f32)


def _nm_matmul(x3, sc, sh, g, ws, name):
    B, T, Dm = x3.shape
    bb, tt = _tile_bt(B, T)
    nt = T // tt
    rows = bb * tt
    in_specs = [pl.BlockSpec((bb, tt, Dm), lambda i, j: (i, j, 0)),
                pl.BlockSpec((bb, 1, Dm), lambda i, j: (i, 0, 0)),
                pl.BlockSpec((bb, 1, Dm), lambda i, j: (i, 0, 0)),
                pl.BlockSpec((1, 1, Dm), lambda i, j: (0, 0, 0))]
    in_specs += [pl.BlockSpec(w.shape, lambda i, j: (0, 0)) for w in ws]
    out_specs = [pl.BlockSpec((rows, w.shape[1]), lambda i, j: (i * nt + j, 0)) for w in ws]
    out_shape = [jax.ShapeDtypeStruct((B * T, w.shape[1]), f32) for w in ws]
    return pl.pallas_call(
        functools.partial(_nm_body, n_w=len(ws)),
        grid=(B // bb, nt),
        in_specs=in_specs, out_specs=out_specs, out_shape=out_shape,
        compiler_params=_cparams(("arbitrary", "arbitrary")),
        name=name,
    )(x3, sc, sh, g, *ws)


def _xnlast_body(x_ref, sc_ref, sh_ref, g_ref, o_ref):
    x = x_ref[...]
    ms = jnp.mean(x * x, axis=-1, keepdims=True)
    xn = x * lax.rsqrt(ms + NORM_EPS) * g_ref[...]
    o_ref[...] = xn * (1.0 + sc_ref[...]) + sh_ref[...]


def _xnlast(x2, sc, sh, g):
    return pl.pallas_call(_xnlast_body, out_shape=jax.ShapeDtypeStruct(x2.shape, f32), name="xn_last")(x2, sc, sh, g)


def _mm_body(x_ref, w_ref, o_ref):
    o_ref[...] = jnp.dot(x_ref[...].astype(bf16), w_ref[...], preferred_element_type=f32)


def _mm(x, w):
    return pl.pallas_call(_mm_body, out_shape=jax.ShapeDtypeStruct((x.shape[0], w.shape[1]), f32),
                          compiler_params=_cparams(None), name="shift_proj")(x, w)


def _rwpre_body(p_ref, prev_ref, mu_ref, w0_ref, w2_ref, a0_ref, a2_ref, g2_ref, kk_ref_, ka_ref, rk_ref, ones_ref,
                r_o, d_o, k_o, v_o, kk_o, b_o, bon_o, g_o):
    p = p_ref[...]
    q = p + (prev_ref[...] - p) * mu_ref[...]
    r = q[:, 0:RW_DIM]
    k = q[:, RW_DIM:2 * RW_DIM]
    v = q[:, 2 * RW_DIM:3 * RW_DIM]
    l2 = q[:, 3 * RW_DIM:3 * RW_DIM + 128]
    lg = q[:, 3 * RW_DIM + 128:3 * RW_DIM + 256]
    ones = ones_ref[...]
    ww = jnp.dot(jnp.tanh(l2), w2_ref[...], precision=HI, preferred_element_type=f32)
    w = -_softplus(-(w0_ref[...] + ww)) - 0.5
    ew = jnp.exp(w)
    aa = jnp.dot(l2, a2_ref[...], precision=HI, preferred_element_type=f32)
    a = _sigmoid(a0_ref[...] + aa)
    g = jnp.dot(_sigmoid(lg), g2_ref[...], precision=HI, preferred_element_type=f32)
    kk = k * kk_ref_[...]
    ss = jnp.dot(kk * kk, ones, precision=HI, preferred_element_type=f32)
    kk = kk / jnp.maximum(jnp.sqrt(ss), 1e-12)
    k2 = k * (1.0 + (a - 1.0) * ka_ref[...])
    rk = jnp.dot(r * k2 * rk_ref[...], ones, precision=HI, preferred_element_type=f32)
    r_o[...] = r
    d_o[...] = ew
    k_o[...] = k2
    v_o[...] = v
    kk_o[...] = kk
    b_o[...] = kk * a
    bon_o[...] = rk * v
    g_o[...] = g


def _rwpre(p, prev, prm):
    N = p.shape[0]
    tm = math.gcd(N, 256)
    row = lambda c: pl.BlockSpec((tm, c), lambda i: (i, 0))
    full = lambda a: pl.BlockSpec(a.shape, lambda i: (0, 0))
    params = [prm["mu"], prm["w0"], prm["w2p"], prm["a0"], prm["a2p"], prm["g2"], prm["k_k"], prm["k_a"], prm["r_k"],
              prm["ones64"]]
    return pl.pallas_call(
        _rwpre_body,
        grid=(N // tm,),
        in_specs=[row(RW_COLS), row(RW_COLS)] + [full(a) for a in params],
        out_specs=[row(RW_DIM)] * 8,
        out_shape=[jax.ShapeDtypeStruct((N, RW_DIM), f32)] * 8,
        compiler_params=_cparams(("arbitrary",)),
        name="rwkv_pre",
    )(p, prev, *params)


def _segsum(x, ones):
    hi = x.astype(bf16)
    lo = (x - hi.astype(f32)).astype(bf16)
    return (jnp.dot(hi, ones, preferred_element_type=f32) + jnp.dot(lo, ones, preferred_element_type=f32))


def _scan_body(r_ref, d_ref, k_ref, v_ref, kk_ref, b_ref, s0_ref, y_ref, sT_ref, S, pm_s, p2_s, *, bb, Tc, ntc):
    tci = pl.program_id(1)
    npair = RW_HEADS // 2
    nblk = bb * npair
    R = nblk * RW_HEAD

    @pl.when(tci == 0)
    def _():
        for blk in range(nblk):
            b, p = divmod(blk, npair)
            S[blk * 64:(blk + 1) * 64, :] = jnp.concatenate([s0_ref[b, 2 * p], s0_ref[b, 2 * p + 1]], axis=-1)

    ri = lax.broadcasted_iota(jnp.int32, (LANES, LANES), 0)
    ci = lax.broadcasted_iota(jnp.int32, (LANES, LANES), 1)
    ones = jnp.where((ri // 64) == (ci // 64), 1.0, 0.0).astype(bf16)
    r64 = lax.broadcasted_iota(jnp.int32, (RW_HEAD, LANES), 0)
    c64 = lax.broadcasted_iota(jnp.int32, (RW_HEAD, LANES), 1)
    eye2 = jnp.where((c64 % 64) == r64, 1.0, 0.0).astype(f32)

    def step(t, carry):
        row = lambda ref: [ref[b, pl.ds(t, 1), :] for b in range(bb)]
        kk_t, v_t, b_t, k_t, r_t = row(kk_ref), row(v_ref), row(b_ref), row(k_ref), row(r_ref)
        d_t = [jnp.exp(-x) for x in row(d_ref)]
        for blk in range(nblk):
            b, p = divmod(blk, npair)
            rows = slice(blk * 64, (blk + 1) * 64)
            ln = slice(p * LANES, (p + 1) * LANES)
            pm_s[rows, :] = S[rows, :] * kk_t[b][:, ln]
            pm_s[R + blk * 64:R + (blk + 1) * 64, :] = eye2 * v_t[b][:, ln]
        sv = _segsum(pm_s[...], ones)
        for blk in range(nblk):
            b, p = divmod(blk, npair)
            rows = slice(blk * 64, (blk + 1) * 64)
            ln = slice(p * LANES, (p + 1) * LANES)
            sn = (S[rows, :] * d_t[b][:, ln]
                  - sv[blk * 64:(blk + 1) * 64, :] * b_t[b][:, ln]
                  + sv[R + blk * 64:R + (blk + 1) * 64, :] * k_t[b][:, ln])
            S[rows, :] = sn
            p2_s[rows, :] = sn * r_t[b][:, ln]
        yb = _segsum(p2_s[...], ones)
        for b in range(bb):
            yrow = [jnp.sum(eye2 * yb[(b * npair + p) * 64:(b * npair + p + 1) * 64, :], axis=0, keepdims=True)
                    for p in range(npair)]
            y_ref[b, pl.ds(t, 1), :] = jnp.concatenate(yrow, axis=-1)
        return carry

    lax.fori_loop(0, Tc, step, 0)

    @pl.when(tci == ntc - 1)
    def _():
        for blk in range(nblk):
            b, p = divmod(blk, npair)
            s = S[blk * 64:(blk + 1) * 64, :]
            sT_ref[b, 2 * p] = s[:, :64]
            sT_ref[b, 2 * p + 1] = s[:, 64:]


def _scan(r, d, k, v, kk, b, s0):
    B, T, _ = r.shape
    bb = min(SCAN_BB, B)
    Tc = min(64, T)
    ntc = T // Tc
    R = bb * (RW_HEADS // 2) * RW_HEAD
    tok = pl.BlockSpec((bb, Tc, RW_DIM), lambda i, j: (i, j, 0))
    st = pl.BlockSpec((bb, RW_HEADS, RW_HEAD, RW_HEAD), lambda i, j: (i, 0, 0, 0))
    return pl.pallas_call(
        functools.partial(_scan_body, bb=bb, Tc=Tc, ntc=ntc),
        grid=(B // bb, ntc),
        in_specs=[tok] * 6 + [st],
        out_specs=[tok, st],
        out_shape=[jax.ShapeDtypeStruct((B, T, RW_DIM), f32),
                   jax.ShapeDtypeStruct((B, RW_HEADS, RW_HEAD, RW_HEAD), f32)],
        scratch_shapes=[pltpu.VMEM((R, LANES), f32), pltpu.VMEM((2 * R, LANES), f32), pltpu.VMEM((R, LANES), f32)],
        compiler_params=_cparams(("arbitrary", "arbitrary")),
        name="rwkv_scan",
    )(r, d, k, v, kk, b, s0)


RW_CHUNK = 64
RW_CHUNK_BB = 2
RW_PASSES_STATE = 3
RW_PASSES_LOCAL = 1


def _split(x):
    hi = x.astype(bf16)
    return hi, (x - hi.astype(f32)).astype(bf16)


def _mmp(a, b, dims, passes):
    dn = (dims, ((), ()))
    dg = lambda x, y: lax.dot_general(x, y, dn, preferred_element_type=f32)
    if passes == 1:
        return dg(a.astype(bf16), b.astype(bf16))
    a_hi, a_lo = _split(a)
    b_hi, b_lo = _split(b)
    out = dg(a_hi, b_hi) + dg(a_lo, b_hi)
    if passes >= 3:
        out = out + dg(a_hi, b_lo)
    return out


_NN = ((1,), (0,))
_NT = ((1,), (1,))
_TN = ((0,), (0,))


def _rwchunk_body(r_ref, e_ref, k_ref, v_ref, kk_ref, b_ref, s0_ref, y_ref, sT_ref, sbd, *, C, nck, bb):
    ci = pl.program_id(1)
    npair = RW_HEADS // 2
    H2 = 2 * C
    lane = lax.broadcasted_iota(jnp.int32, (C, LANES), 1)
    m0 = lane < RW_HEAD
    rr = lax.broadcasted_iota(jnp.int32, (H2, H2), 0)
    cc = lax.broadcasted_iota(jnp.int32, (H2, H2), 1)
    same = (rr // C) == (cc // C)
    strict = same & ((cc % C) < (rr % C))
    incl = same & ((cc % C) <= (rr % C))
    r128 = lax.broadcasted_iota(jnp.int32, (LANES, LANES), 0)
    c128 = lax.broadcasted_iota(jnp.int32, (LANES, LANES), 1)
    eye = jnp.where(r128 == c128, 1.0, 0.0).astype(f32)
    blk = (r128 // RW_HEAD) == (c128 // RW_HEAD)

    @pl.when(ci == 0)
    def _():
        z = jnp.zeros((RW_HEAD, RW_HEAD), f32)
        for bi in range(bb):
            for p in range(npair):
                top = jnp.concatenate([s0_ref[bi, 2 * p], z], axis=1)
                bot = jnp.concatenate([z, s0_ref[bi, 2 * p + 1]], axis=1)
                sbd[bi * npair + p] = jnp.concatenate([top, bot], axis=0).T

    tr = lax.broadcasted_iota(jnp.int32, (C, C), 0)
    tc = lax.broadcasted_iota(jnp.int32, (C, C), 1)
    tril = jnp.where(tr >= tc, 1.0, 0.0).astype(f32)
    stack = lambda x: jnp.concatenate([jnp.where(m0, x, 0.0), jnp.where(m0, 0.0, x)], axis=0)
    chains = range(bb * npair)
    ar, bk, v_, pc, kc, bc = [], [], [], [], [], []
    for bp in chains:
        bi, p = divmod(bp, npair)
        if p == 0:
            e_all = e_ref[bi]
            cum_all = jnp.dot(tril, e_all, precision=HI, preferred_element_type=f32)
        ln = slice(p * LANES, (p + 1) * LANES)
        cum = cum_all[:, ln]
        pt = jnp.exp(-cum)
        pi = jnp.exp(cum)
        pm = jnp.exp(e_all[:, ln] - cum)
        b_ = stack(pi * b_ref[bi, :, ln])
        k_ = stack(pi * k_ref[bi, :, ln])
        ar.append(jnp.concatenate([stack(pm * kk_ref[bi, :, ln]), stack(pt * r_ref[bi, :, ln])], axis=0))
        bk.append(jnp.concatenate([b_, k_], axis=0))
        v_.append(stack(v_ref[bi, :, ln]))
        pc.append(pt[C - 1:C, :])
        kc.append(k_ * pc[-1])
        bc.append(b_ * pc[-1])
    big = [_mmp(ar[c], bk[c], _NT, RW_PASSES_LOCAL) for c in chains]
    s = [sbd[c] for c in chains]
    g0 = [_mmp(ar[c], s[c], _NN, RW_PASSES_STATE) for c in chains]
    lk = [jnp.where(strict, big[c][:H2, H2:], 0.0) for c in chains]
    w = [g0[c][:H2] + _mmp(lk[c], v_[c], _NN, RW_PASSES_LOCAL) for c in chains]
    x = [jnp.where(strict, -big[c][:H2, :H2], 0.0) for c in chains]
    nfac = C.bit_length() - 1
    for j in range(nfac):
        w = [w[c] + _mmp(x[c], w[c], _NN, RW_PASSES_LOCAL) for c in chains]
        if j < nfac - 1:
            x = [_mmp(x[c], x[c], _NN, RW_PASSES_LOCAL) for c in chains]
    mk = [jnp.where(incl, big[c][H2:, H2:], 0.0) for c in chains]
    mb = [jnp.where(incl, big[c][H2:, :H2], 0.0) for c in chains]
    ya = [g0[c][H2:] + _mmp(mk[c], v_[c], _NN, RW_PASSES_LOCAL) for c in chains]
    yb = [_mmp(mb[c], w[c], _NN, RW_PASSES_LOCAL) for c in chains]
    up = [_mmp(kc[c], v_[c], _TN, RW_PASSES_STATE) for c in chains]
    um = [_mmp(bc[c], w[c], _TN, RW_PASSES_STATE) for c in chains]
    for c in chains:
        bi, p = divmod(c, npair)
        yy = ya[c] - yb[c]
        y_ref[bi, :, p * LANES:(p + 1) * LANES] = yy[:C] + yy[C:]
        col = jnp.sum(eye * pc[c], axis=1, keepdims=True)
        sbd[c] = s[c] * col + jnp.where(blk, up[c] - um[c], 0.0)

    @pl.when(ci == nck - 1)
    def _():
        for bi in range(bb):
            for p in range(npair):
                m = sbd[bi * npair + p].T
                sT_ref[bi, 2 * p] = m[:RW_HEAD, :RW_HEAD]
                sT_ref[bi, 2 * p + 1] = m[RW_HEAD:, RW_HEAD:]


def _rwchunk(r, e, k, v, kk, b, s0):
    B, T, _ = r.shape
    C = RW_CHUNK
    nck = T // C
    bb = math.gcd(B, RW_CHUNK_BB)
    tok = pl.BlockSpec((bb, C, RW_DIM), lambda i, j: (i, j, 0))
    st = pl.BlockSpec((bb, RW_HEADS, RW_HEAD, RW_HEAD), lambda i, j: (i, 0, 0, 0))
    return pl.pallas_call(
        functools.partial(_rwchunk_body, C=C, nck=nck, bb=bb),
        grid=(B // bb, nck),
        in_specs=[tok] * 6 + [st],
        out_specs=[tok, st],
        out_shape=[jax.ShapeDtypeStruct((B, T, RW_DIM), f32),
                   jax.ShapeDtypeStruct((B, RW_HEADS, RW_HEAD, RW_HEAD), f32)],
        scratch_shapes=[pltpu.VMEM((bb * (RW_HEADS // 2), LANES, LANES), f32)],
        compiler_params=_cparams(("arbitrary", "arbitrary")),
        name="rwkv_chunk",
    )(r, e, k, v, kk, b, s0)


def _mamba_body(z_ref, xbc_ref, dt_ref, dtT_ref, conv0_ref, ssm0_ref, cw_ref, cb_ref, dtb_ref, dtbT_ref, alog_ref,
                alogT_ref, dexp_ref, nw_ref, ob_ref, convT_ref, ssmT_ref, ext, hst, ybuf, *, L, nc):
    c = pl.program_id(1)
    CD = SSM_CONV_DIM

    @pl.when(c == 0)
    def _():
        ext[0:8, :] = jnp.zeros((8, CD), f32)
        ext[5:8, :] = conv0_ref[0]
        hst[...] = ssm0_ref[0]

    ext[8:8 + L, :] = xbc_ref[...]
    cw = cw_ref[...]
    conv = (cb_ref[...] + ext[5:5 + L, :] * cw[0:1] + ext[6:6 + L, :] * cw[1:2]
            + ext[7:7 + L, :] * cw[2:3] + ext[8:8 + L, :] * cw[3:4])
    xc = conv * _sigmoid(conv)

    @pl.when(c == nc - 1)
    def _():
        convT_ref[0] = ext[L + 5:L + 8, :]

    ext[0:8, :] = ext[L:L + 8, :]

    dtc = _softplus(dt_ref[...] + dtb_ref[...])
    a_col = dtc * (-jnp.exp(alog_ref[...]))
    dtr = _softplus(dtT_ref[0] + dtbT_ref[...])
    a_row = dtr * (-jnp.exp(alogT_ref[...]))
    ri = lax.broadcasted_iota(jnp.int32, (L, L), 0)
    ci = lax.broadcasted_iota(jnp.int32, (L, L), 1)
    lower = ri >= ci
    cs_col = jnp.dot(jnp.where(lower, 1.0, 0.0).astype(f32), a_col, precision=HI, preferred_element_type=f32)
    cs_row = jnp.dot(a_row, jnp.where(ri <= ci, 1.0, 0.0).astype(f32), precision=HI, preferred_element_type=f32)

    xs = xc[:, :SSM_DIM]
    gn = SSM_GROUPS * SSM_STATE
    hpg = SSM_HEADS // SSM_GROUPS
    for g in range(SSM_GROUPS):
        Bb = xc[:, SSM_DIM + g * SSM_STATE:SSM_DIM + (g + 1) * SSM_STATE].astype(bf16)
        Cb = xc[:, SSM_DIM + gn + g * SSM_STATE:SSM_DIM + gn + (g + 1) * SSM_STATE].astype(bf16)
        cbm = _nt(Cb, Bb)
        for j in range(hpg):
            h = g * hpg + j
            col = cs_col[:, h:h + 1]
            row = cs_row[h:h + 1, :]
            ldec = jnp.exp(jnp.where(lower, col - row, NEG))
            gm = cbm * ldec * dtr[h:h + 1, :]
            xh = xs[:, h * SSM_HEAD:(h + 1) * SSM_HEAD]
            yd = jnp.dot(gm.astype(bf16), xh.astype(bf16), preferred_element_type=f32)
            hp = hst[h]
            yo = jnp.exp(col) * _nt(Cb, hp.astype(bf16))
            last = row[:, L - 1:L]
            wcol = jnp.exp(last - col) * dtc[:, h:h + 1]
            hst[h] = jnp.exp(last) * hp + _tn((xh * wcol).astype(bf16), Bb)
            ybuf[:, h * SSM_HEAD:(h + 1) * SSM_HEAD] = yd + yo

    z = z_ref[...]
    y = (ybuf[...] + xs * dexp_ref[...]) * (z * _sigmoid(z))
    gw = SSM_DIM // SSM_GROUPS
    for g in range(SSM_GROUPS):
        yg = y[:, g * gw:(g + 1) * gw]
        ms = jnp.mean(yg * yg, axis=-1, keepdims=True)
        ob_ref[:, g * gw:(g + 1) * gw] = yg * lax.rsqrt(ms + NORM_EPS) * nw_ref[:, g * gw:(g + 1) * gw]

    @pl.when(c == nc - 1)
    def _():
        ssmT_ref[0] = hst[...]


def _mamba(z, xbc, dt, dtT, conv0, ssm0, prm, B, T):
    L =math.gcd(T, SSM_CHUNK)
    nc = T // L
    row = lambda cdim: pl.BlockSpec((L, cdim), lambda b, c: (b * nc + c, 0))
    full = lambda a: pl.BlockSpec(a.shape, lambda b, c: (0,) * a.ndim)
    params = [prm["conv_w"], prm["conv_b"], prm["dt_bias"], prm["dt_biasT"], prm["A_log"], prm["A_logT"],
              prm["D_exp"], prm["norm_w"]]
    return pl.pallas_call(
        functools.partial(_mamba_body, L=L, nc=nc),
        grid=(B, nc),
        in_specs=[row(SSM_DIM), row(SSM_CONV_DIM), row(LANES),
                  pl.BlockSpec((1, SSM_HEADS, L), lambda b, c: (b, 0, c)),
                  pl.BlockSpec((1, SSM_CONV - 1, SSM_CONV_DIM), lambda b, c: (b, 0, 0)),
                  pl.BlockSpec((1, SSM_HEADS, SSM_HEAD, SSM_STATE), lambda b, c: (b, 0, 0, 0))]
                 + [full(a) for a in params],
        out_specs=[row(SSM_DIM),
                   pl.BlockSpec((1, SSM_CONV - 1, SSM_CONV_DIM), lambda b, c: (b, 0, 0)),
                   pl.BlockSpec((1, SSM_HEADS, SSM_HEAD, SSM_STATE), lambda b, c: (b, 0, 0, 0))],
        out_shape=[jax.ShapeDtypeStruct((B * T, SSM_DIM), f32),
                   jax.ShapeDtypeStruct((B, SSM_CONV - 1, SSM_CONV_DIM), f32),
                   jax.ShapeDtypeStruct((B, SSM_HEADS, SSM_HEAD, SSM_STATE), f32)],
        scratch_shapes=[pltpu.VMEM((L + 8, SSM_CONV_DIM), f32),
                        pltpu.VMEM((SSM_HEADS, SSM_HEAD, SSM_STATE), f32),
                        pltpu.VMEM((L, SSM_DIM), f32)],
        compiler_params=_cparams(("arbitrary", "arbitrary")),
        name="mamba",
    )(z, xbc, dt, dtT, conv0, ssm0, *params)


def _merge_body(y_ref, bon_ref, g_ref, ob_ref, gate_ref, x_ref, gt1_ref, sc2_ref, sh2_ref, lnw_ref, lnb_ref, ones_ref,
                wpa_ref, wpb_ref, wout_ref, n2g_ref, x1_ref, xn2_ref):
    y = y_ref[...]
    ones = ones_ref[...]
    inv = 1.0 / RW_HEAD
    mean = jnp.dot(y, ones, precision=HI, preferred_element_type=f32) * inv
    yc = y - mean
    var = jnp.dot(yc * yc, ones, precision=HI, preferred_element_type=f32) * inv
    yn = yc * lax.rsqrt(var + RW_LN_EPS) * lnw_ref[...] + lnb_ref[...]
    oa = (yn + bon_ref[...]) * g_ref[...]
    gate = _sigmoid(gate_ref[...])
    ma = jnp.dot(oa.astype(bf16), wpa_ref[...], preferred_element_type=f32)
    mb = jnp.dot(ob_ref[...].astype(bf16), wpb_ref[...], preferred_element_type=f32)
    merged = gate[:, :D_MODEL] * ma + gate[:, D_MODEL:] * mb
    mo = jnp.dot(merged.astype(bf16), wout_ref[...], preferred_element_type=f32)
    x = x_ref[...]
    bb, tt, Dm = x.shape
    x1 = x + gt1_ref[...] * mo.reshape(bb, tt, Dm)
    x1_ref[...] = x1.reshape(bb * tt, Dm)
    ms = jnp.mean(x1 * x1, axis=-1, keepdims=True)
    xn2 = x1 * lax.rsqrt(ms + NORM_EPS) * n2g_ref[...]
    xn2 = xn2 * (1.0 + sc2_ref[...]) + sh2_ref[...]
    xn2_ref[...] = xn2.reshape(bb * tt, Dm).astype(bf16)


def _merge(y, bon, g, ob, gate, x3, gt1, sc2, sh2, prm):
    B, T, Dm = x3.shape
    bb, tt = _tile_bt(B, T)
    if tt == 512:
        tt = 256
    nt = T // tt
    rows = bb * tt
    row = lambda c: pl.BlockSpec((rows, c), lambda i, j: (i * nt + j, 0))
    mod = pl.BlockSpec((bb, 1, Dm), lambda i, j: (i, 0, 0))
    full = lambda a: pl.BlockSpec(a.shape, lambda i, j: (0,) * a.ndim)
    params = [prm["ln_w"], prm["ln_b"], prm["ones64"], prm["w_pa"], prm["w_pb"], prm["w_out"], prm["norm2_g"]]
    return pl.pallas_call(
        _merge_body,
        grid=(B // bb, nt),
        in_specs=[row(RW_DIM), row(RW_DIM), row(RW_DIM), row(SSM_DIM), row(2 * D_MODEL),
                  pl.BlockSpec((bb, tt, Dm), lambda i, j: (i, j, 0)), mod, mod, mod] + [full(a) for a in params],
        out_specs=[row(Dm), row(Dm)],
        out_shape=[jax.ShapeDtypeStruct((B * T, Dm), f32), jax.ShapeDtypeStruct((B * T, Dm), bf16)],
        compiler_params=_cparams(("arbitrary", "arbitrary")),
        name="merge",
    )(y, bon, g, ob, gate, x3, gt1, sc2, sh2, *params)


def _peer_body(xb_ref, wqT_ref, keys_ref, u_ref, vT_ref, o_ref, qT_s, sc_s, cnt_s, a_s, r2_s, b_s, t_s, cand_s, acc_s,
               w_s, *, ne):
    e = pl.program_id(1)
    xb = xb_ref[...]
    TM = xb.shape[0]
    K = PEER_TOPK
    NK = PEER_NKEYS
    half = PEER_DKEY // 2
    nlt = TM // LANES
    ncand = sum(K // (a + 1) for a in range(K))

    @pl.when(e == 0)
    def _():
        qT_s[...] = _nt(wqT_ref[...], xb)
        cand_s[...] = jnp.full(cand_s.shape, NEG, f32)

        def head(h, carry):
            for c in range(2):
                lo = pl.multiple_of(h * PEER_DKEY + c * half, half)
                qh = qT_s[pl.ds(lo, half), :].astype(bf16)
                sc_s[c] = jnp.dot(keys_ref[h, c], qh, preferred_element_type=f32)
            for lt in range(nlt):
                ln = slice(lt * LANES, (lt + 1) * LANES)
                s1 = sc_s[0, :, ln]
                s2 = sc_s[1, :, ln]
                work = s1
                for it in range(K):
                    m = jnp.max(work, axis=0, keepdims=True)
                    t_s[it:it + 1, ln] = m
                    work = jnp.where(work == m, NEG, work)
                work = s2
                r2 = jnp.full((NK, LANES), float(K), f32)
                for it in range(K):
                    m = jnp.max(work, axis=0, keepdims=True)
                    t_s[K + it:K + it + 1, ln] = m
                    hit = work == m
                    r2 = jnp.where(hit, float(it), r2)
                    work = jnp.where(hit, NEG, work)
                idx = 0
                for a in range(K):
                    nb = K // (a + 1)
                    cand_s[idx:idx + nb, ln] = t_s[a:a + 1, ln] + t_s[K:K + nb, ln]
                    idx += nb
                cand = cand_s[:, ln]
                rank = jnp.zeros(cand.shape, f32)
                for j in range(ncand):
                    rank = rank + jnp.where(cand_s[j:j + 1, ln] > cand, 1.0, 0.0)
                tau = jnp.min(jnp.where(rank <= K - 1.0, cand, -NEG), axis=0, keepdims=True)
                mx = cand_s[0:1, ln]
                zsum = jnp.sum(jnp.where(cand >= tau, jnp.exp(cand - mx), 0.0), axis=0, keepdims=True)
                t1 = t_s[0:K, ln]
                cnt = jnp.zeros((NK, LANES), f32)
                for b in range(K):
                    ok = (t1 + t_s[K + b:K + b + 1, ln]) >= tau
                    theta = jnp.min(jnp.where(ok, t1, -NEG), axis=0, keepdims=True)
                    cnt = cnt + jnp.where(s1 >= theta, 1.0, 0.0)
                cnt_s[h, :, ln] = cnt
                a_s[h, :, ln] = jnp.exp(s1 - t_s[0:1, ln])
                r2_s[h, :, :, ln] = r2.astype(bf16).reshape(NK // 16, 16, LANES)
                bv = jnp.exp(s2 - t_s[K:K + 1, ln]) / zsum
                b_s[h, :, :, ln] = bv.astype(bf16).reshape(NK // 16, 16, LANES)
            return carry

        lax.fori_loop(0, PEER_HEADS, head, 0)
        acc_s[...] = jnp.zeros(acc_s.shape, f32)

    at = _nt(u_ref[...], xb)
    nsub = PEER_EB // NK
    for ii in range(nsub):
        i = e * nsub + ii
        gsum = jnp.zeros((NK // 16, 16, TM), bf16)
        for h in range(PEER_HEADS):
            crow = jnp.broadcast_to(cnt_s[h, pl.ds(i, 1), :], (16, TM)).astype(bf16)[None]
            arow = jnp.broadcast_to(a_s[h, pl.ds(i, 1), :], (16, TM)).astype(bf16)[None]
            gsum = gsum + jnp.where(r2_s[h] < crow, b_s[h], jnp.zeros((), bf16)) * arow
        av = at[ii * NK:(ii + 1) * NK, :]
        act = 0.5 * av * (1.0 + lax.erf(av * INV_SQRT2))
        w_s[ii * NK:(ii + 1) * NK, :] = gsum.reshape(NK, TM) * act.astype(bf16)
    acc_s[...] += jnp.dot(vT_ref[...], w_s[...], preferred_element_type=f32)

    @pl.when(e == ne - 1)
    def _():
        o_ref[...] = acc_s[...].T


def _peer(xb, prm):
    N, Dm = xb.shape
    TM = PEER_TM
    ne = (PEER_NKEYS * PEER_NKEYS) // PEER_EB
    K = PEER_TOPK
    return pl.pallas_call(
        functools.partial(_peer_body, ne=ne),
        grid=(N // TM, ne),
        in_specs=[pl.BlockSpec((TM, Dm), lambda i, e: (i, 0)),
                  pl.BlockSpec(prm["wqT"].shape, lambda i, e: (0, 0)),
                  pl.BlockSpec(prm["keys"].shape, lambda i, e: (0, 0, 0, 0)),
                  pl.BlockSpec((PEER_EB, Dm), lambda i, e: (e, 0)),
                  pl.BlockSpec((Dm, PEER_EB), lambda i, e: (0, e))],
        out_specs=pl.BlockSpec((TM, Dm), lambda i, e: (i, 0)),
        out_shape=jax.ShapeDtypeStruct((N, Dm), f32),
        scratch_shapes=[pltpu.VMEM((PEER_HEADS * PEER_DKEY, TM), f32), pltpu.VMEM((2, PEER_NKEYS, TM), f32),
                        pltpu.VMEM((PEER_HEADS, PEER_NKEYS, TM), f32), pltpu.VMEM((PEER_HEADS, PEER_NKEYS, TM), f32),
                        pltpu.VMEM((PEER_HEADS, PEER_NKEYS // 16, 16, TM), bf16),
                        pltpu.VMEM((PEER_HEADS, PEER_NKEYS // 16, 16, TM), bf16),
                        pltpu.VMEM((2 * K, TM), f32), pltpu.VMEM((56, TM), f32), pltpu.VMEM((Dm, TM), f32),
                        pltpu.VMEM((PEER_EB, TM), bf16)],
        compiler_params=_cparams(("arbitrary", "arbitrary")),
        name="peer",
    )(xb, prm["wqT"], prm["keys"], prm["u"], prm["vT"])


def _final_body(x1_ref, pe_ref, gt2_ref, scf_ref, shf_ref, fg_ref, o_ref):
    bb, tt, Dm = o_ref.shape
    x2 = x1_ref[...].reshape(bb, tt, Dm) + gt2_ref[...] * pe_ref[...].reshape(bb, tt, Dm)
    ms = jnp.mean(x2 * x2, axis=-1, keepdims=True)
    y = x2 * lax.rsqrt(ms + NORM_EPS) * fg_ref[...]
    o_ref[...] = y * (1.0 + scf_ref[...]) + shf_ref[...]


def _final(x1, pe, gt2, scf, shf, fg, B, T):
    Dm = x1.shape[1]
    bb, tt = _tile_bt(B, T)
    nt = T // tt
    row = pl.BlockSpec((bb * tt, Dm), lambda i, j: (i * nt + j, 0))
    mod = pl.BlockSpec((bb, 1, Dm), lambda i, j: (i, 0, 0))
    return pl.pallas_call(
        _final_body,
        grid=(B // bb, nt),
        in_specs=[row, row, mod, mod, mod, pl.BlockSpec((1, 1, Dm), lambda i, j: (0, 0, 0))],
        out_specs=pl.BlockSpec((bb, tt, Dm), lambda i, j: (i, j, 0)),
        out_shape=jax.ShapeDtypeStruct((B, T, Dm), f32),
        compiler_params=_cparams(("arbitrary", "arbitrary")),
        name="final_norm",
    )(x1, pe, gt2, scf, shf, fg)


def _prepare(w_in, rw_mu, rw_w0, rw_w2, rw_a0, rw_a2, rw_g2, rw_k_k, rw_k_a, rw_r_k, rw_ln_w, rw_ln_b, ssm_conv_w,
             ssm_conv_b, ssm_dt_bias, ssm_A_log, ssm_D, ssm_norm_w, w_pa, w_pb, w_out, norm1_g, norm2_g, peer_wq,
             peer_keys, peer_u, peer_v, l):
    w = w_in[l]
    o1 = RW_COLS
    o2 = o1 + SSM_DIM
    o3 = o2 + SSM_CONV_DIM
    o4 = o3 + SSM_HEADS
    seg = lax.broadcasted_iota(jnp.int32, (RW_DIM, RW_DIM), 0) // RW_HEAD
    seg_c = lax.broadcasted_iota(jnp.int32, (RW_DIM, RW_DIM), 1) // RW_HEAD
    zlora = jnp.zeros((64, RW_DIM), f32)
    pad16 = lambda v: jnp.pad(v.reshape(1, SSM_HEADS), ((0, 0), (0, LANES - SSM_HEADS)))
    return dict(
        w_rw=w[:, :o1].astype(bf16), w_z=w[:, o1:o2].astype(bf16), w_xbc=w[:, o2:o3].astype(bf16),
        w_dt=jnp.pad(w[:, o3:o4], ((0, 0), (0, LANES - SSM_HEADS))).astype(bf16), w_gate=w[:, o4:].astype(bf16),
        norm1_g=norm1_g[l].reshape(1, 1, D_MODEL), norm1_g2=norm1_g[l].reshape(1, D_MODEL),
        mu=rw_mu[l].reshape(1, RW_COLS), w0=rw_w0[l].reshape(1, RW_DIM), a0=rw_a0[l].reshape(1, RW_DIM),
        w2p=jnp.concatenate([rw_w2[l], zlora], axis=0), a2p=jnp.concatenate([zlora, rw_a2[l]], axis=0), g2=rw_g2[l],
        k_k=rw_k_k[l].reshape(1, RW_DIM), k_a=rw_k_a[l].reshape(1, RW_DIM), r_k=rw_r_k[l].reshape(1, RW_DIM),
        ln_w=rw_ln_w[l].reshape(1, RW_DIM), ln_b=rw_ln_b[l].reshape(1, RW_DIM),
        ones64=jnp.where(seg == seg_c, 1.0, 0.0).astype(f32),
        conv_w=ssm_conv_w[l], conv_b=ssm_conv_b[l].reshape(1, SSM_CONV_DIM),
        dt_bias=pad16(ssm_dt_bias[l]), dt_biasT=ssm_dt_bias[l].reshape(SSM_HEADS, 1),
        A_log=pad16(ssm_A_log[l]), A_logT=ssm_A_log[l].reshape(SSM_HEADS, 1),
        D_exp=jnp.repeat(ssm_D[l], SSM_HEAD).reshape(1, SSM_DIM), norm_w=ssm_norm_w[l].reshape(1, SSM_DIM),
        w_pa=w_pa[l].astype(bf16), w_pb=w_pb[l].astype(bf16), w_out=w_out[l].astype(bf16),
        norm2_g=norm2_g[l].reshape(1, 1, D_MODEL),
        wqT=peer_wq[l].T.astype(bf16), keys=peer_keys[l].astype(bf16), u=peer_u[l].astype(bf16),
        vT=peer_v[l].T.astype(bf16),
    )


def _layer_pre(x, mod, shift0, wkv0, conv0, ssm0, prm):
    B, T, Dm = x.shape
    N = B * T
    m3 = lambda i: mod[:, i * Dm:(i + 1) * Dm].reshape(B, 1, Dm)
    sh1, sc1, gt1, sh2, sc2, gt2 = [m3(i) for i in range(6)]
    p_rw, dt = _nm_matmul(x, sc1, sh1, prm["norm1_g"], [prm["w_rw"], prm["w_dt"]], "in_proj_rw")
    z, xbc = _nm_matmul(x, sc1, sh1, prm["norm1_g"], [prm["w_z"], prm["w_xbc"]], "in_proj_ssm")
    (gate,) = _nm_matmul(x, sc1, sh1, prm["norm1_g"], [prm["w_gate"]], "in_proj_gate")
    xn_last = _xnlast(x[:, -1, :], sc1[:, 0], sh1[:, 0], prm["norm1_g2"])
    p_prev = _mm(shift0, prm["w_rw"])
    p3 = p_rw.reshape(B, T, RW_COLS)
    prev = jnp.concatenate([p_prev[:, None, :], p3[:, :-1, :]], axis=1).reshape(N, RW_COLS)
    r, d, k, v, kk, b, bon, g = _rwpre(p_rw, prev, prm)
    t3 = lambda a: a.reshape(B, T, RW_DIM)
    rw_scan = _rwchunk if T % RW_CHUNK == 0 else _scan
    y, wkvT = rw_scan(t3(r), t3(d), t3(k), t3(v), t3(kk), t3(b), wkv0)
    dtT = jnp.transpose(dt[:, :SSM_HEADS].reshape(B, T, SSM_HEADS), (0, 2, 1))
    ob, convT, ssmT = _mamba(z, xbc, dt, dtT, conv0, ssm0, prm, B, T)
    x1, xn2 = _merge(y.reshape(N, RW_DIM), bon, g, ob, gate, x, gt1, sc2, sh2, prm)
    return x1, xn2, gt2, xn_last, wkvT, convT, ssmT


def _trunk(xs, cs, states, prm, w_ada, b_ada, final_g, w_ada_f, b_ada_f):
    Dm = xs[0].shape[-1]
    nb = [x.shape[0] for x in xs]
    offs = [sum(nb[:i]) for i in range(len(xs) + 1)]
    c_all = jnp.concatenate(cs, axis=0)
    mod = _ada(c_all, w_ada, b_ada.reshape(1, -1))
    fm = _ada(c_all, w_ada_f, b_ada_f.reshape(1, -1))
    pre = [_layer_pre(x, mod[offs[i]:offs[i + 1]], *states[i], prm) for i, x in enumerate(xs)]
    xn2_all = jnp.concatenate([p[1] for p in pre], axis=0)
    n_all = xn2_all.shape[0]
    n_pad = -(-n_all // PEER_TM) * PEER_TM
    if n_pad != n_all:
        xn2_all = jnp.pad(xn2_all, ((0, n_pad - n_all), (0, 0)))
    pe_all = _peer(xn2_all, prm)
    outs = []
    tok = 0
    for i, x in enumerate(xs):
        B, T, _ = x.shape
        x1, _, gt2, xn_last, wkvT, convT, ssmT = pre[i]
        m = fm[offs[i]:offs[i + 1]]
        shf = m[:, :Dm].reshape(B, 1, Dm)
        scf = m[:, Dm:].reshape(B, 1, Dm)
        y = _final(x1, pe_all[tok:tok + B * T], gt2, scf, shf, final_g.reshape(1, 1, Dm), B, T)
        tok += B * T
        outs.append((y, xn_last[None], wkvT[None], convT[None], ssmT[None]))
    return outs


def kernel(x_prompt, x_sample, c_prompt, c_sample, state_shift, state_wkv, state_conv, state_ssm, w_ada, b_ada, norm1_g, w_in, rw_mu, rw_w0, rw_w2, rw_a0, rw_a2, rw_g2, rw_k_k, rw_k_a, rw_r_k, rw_ln_w, rw_ln_b, ssm_conv_w, ssm_conv_b, ssm_dt_bias, ssm_A_log, ssm_D, ssm_norm_w, w_pa, w_pb, w_out, norm2_g, peer_wq, peer_keys, peer_u, peer_v, final_g, w_ada_f, b_ada_f):
    assert w_in.shape[0] == 1, "single-layer trunk"
    Bp = x_prompt.shape[0]
    prm = _prepare(w_in, rw_mu, rw_w0, rw_w2, rw_a0, rw_a2, rw_g2, rw_k_k, rw_k_a, rw_r_k, rw_ln_w, rw_ln_b,
                   ssm_conv_w, ssm_conv_b, ssm_dt_bias, ssm_A_log, ssm_D, ssm_norm_w, w_pa, w_pb, w_out,
                   norm1_g, norm2_g, peer_wq, peer_keys, peer_u, peer_v, 0)
    zeros = (jnp.zeros((Bp,) + state_shift.shape[2:], f32), jnp.zeros((Bp,) + state_wkv.shape[2:], f32),
             jnp.zeros((Bp,) + state_conv.shape[2:], f32), jnp.zeros((Bp,) + state_ssm.shape[2:], f32))
    carried = (state_shift[0], state_wkv[0], state_conv[0], state_ssm[0])
    (yp, *sp), (ys, *ss) = _trunk([x_prompt, x_sample], [c_prompt, c_sample], [zeros, carried], prm,
                                  w_ada[0], b_ada[0], final_g, w_ada_f, b_ada_f)
    return (yp, ys, *sp, *ss)
```

```python
import functools
import math

import jax
import jax.numpy as jnp
from jax import lax
from jax.experimental import pallas as pl
from jax.experimental.pallas import tpu as pltpu

f32 = jnp.float32
bf16 = jnp.bfloat16
HI = lax.Precision.HIGHEST

D_MODEL = 1024
RW_HEAD = 64
RW_HEADS = 8
RW_DIM = RW_HEADS * RW_HEAD
RW_COLS = 3 * RW_DIM + 64 + 64 + 128
RW_LN_EPS = 64e-5
SSM_DIM = 1024
SSM_HEAD = 64
SSM_HEADS = 16
SSM_GROUPS = 2
SSM_STATE = 128
SSM_CONV = 4
SSM_CHUNK = 128
SSM_CONV_DIM = SSM_DIM + 2 * SSM_GROUPS * SSM_STATE
PEER_HEADS = 8
PEER_NKEYS = 128
PEER_TOPK = 16
PEER_DKEY = 256
NORM_EPS = 1e-6

LANES = 128
VMEM_LIMIT = 56 * 1024 * 1024
NEG = -1e30
INV_SQRT2 = 0.7071067811865476

PEER_TM = 512
PEER_EB = 1024
SCAN_BB = 8


def _cparams(sem):
    return pltpu.CompilerParams(dimension_semantics=sem, vmem_limit_bytes=VMEM_LIMIT)


def _softplus(x):
    return jnp.maximum(x, 0.0) + jnp.log1p(jnp.exp(-jnp.abs(x)))


def _sigmoid(x):
    return jax.nn.sigmoid(x)


def _nt(a, b):
    return lax.dot_general(a, b, (((1,), (1,)), ((), ())), preferred_element_type=f32)


def _tn(a, b):
    return lax.dot_general(a, b, (((0,), (0,)), ((), ())), preferred_element_type=f32)


def _split(x):
    hi = x.astype(bf16)
    return hi, (x - hi.astype(f32)).astype(bf16)


_NN = ((1,), (0,))
_NT = ((1,), (1,))
_TN = ((0,), (0,))


def _mmp(a, b, dims, passes):
    dn = (dims, ((), ()))
    dg = lambda x, y: lax.dot_general(x, y, dn, preferred_element_type=f32)
    if passes == 1:
        return dg(a.astype(bf16), b.astype(bf16))
    a_hi, a_lo = _split(a)
    b_hi, b_lo = _split(b)
    out = dg(a_hi, b_hi) + dg(a_lo, b_hi)
    if passes >= 3:
        out = out + dg(a_hi, b_lo)
    return out


def _headsum(x, ones):
    hi, lo = _split(x)
    parts = []
    for t in range(x.shape[1] // LANES):
        ln = slice(t * LANES, (t + 1) * LANES)
        parts.append(jnp.dot(hi[:, ln], ones, preferred_element_type=f32)
                     + jnp.dot(lo[:, ln], ones, preferred_element_type=f32))
    return jnp.concatenate(parts, axis=-1)


def _tile_bt(B, T):
    if T >= 512:
        return 1, 512
    bb = max(1, min(B, 256 // T))
    return bb, T


def _ada_body(c_ref, w_ref, b_ref, o_ref):
    c = c_ref[...]
    s = c * _sigmoid(c)
    o_ref[...] = jnp.dot(s, w_ref[...], precision=HI, preferred_element_type=f32) + b_ref[...]


def _ada(c, w, b):
    M, Dm = c.shape
    Nc = w.shape[1]
    tn = 1024
    return pl.pallas_call(
        _ada_body,
        grid=(Nc // tn,),
        in_specs=[pl.BlockSpec((M, Dm), lambda j: (0, 0)),
                  pl.BlockSpec((Dm, tn), lambda j: (0, j)),
                  pl.BlockSpec((1, tn), lambda j: (0, j))],
        out_specs=pl.BlockSpec((M, tn), lambda j: (0, j)),
        out_shape=jax.ShapeDtypeStruct((M, Nc), f32),
        compiler_params=_cparams(("arbitrary",)),
        name="ada",
    )(c, w, b)


def _nm_body(x_ref, sc_ref, sh_ref, g_ref, *refs, n_w):
    w_refs, o_refs = refs[:n_w], refs[n_w:]
    x = x_ref[...]
    bb, tt, Dm = x.shape
    ms = jnp.mean(x * x, axis=-1, keepdims=True)
    xn = x * lax.rsqrt(ms + NORM_EPS) * g_ref[...]
    xn = xn * (1.0 + sc_ref[...]) + sh_ref[...]
    xb = xn.reshape(bb * tt, Dm).astype(bf16)
    for w_ref, o_ref in zip(w_refs, o_refs):
        o_ref[...] = jnp.dot(xb, w_ref[...], preferred_element_type=f32)


def _nm_matmul(x3, sc, sh, g, ws, name):
    B, T, Dm = x3.shape
    bb, tt = _tile_bt(B, T)
    nt = T // tt
    rows = bb * tt
    in_specs = [pl.BlockSpec((bb, tt, Dm), lambda i, j: (i, j, 0)),
                pl.BlockSpec((bb, 1, Dm), lambda i, j: (i, 0, 0)),
                pl.BlockSpec((bb, 1, Dm), lambda i, j: (i, 0, 0)),
                pl.BlockSpec((1, 1, Dm), lambda i, j: (0, 0, 0))]
    in_specs += [pl.BlockSpec(w.shape, lambda i, j: (0, 0)) for w in ws]
    out_specs = [pl.BlockSpec((rows, w.shape[1]), lambda i, j: (i * nt + j, 0)) for w in ws]
    out_shape = [jax.ShapeDtypeStruct((B * T, w.shape[1]), f32) for w in ws]
    return pl.pallas_call(
        functools.partial(_nm_body, n_w=len(ws)),
        grid=(B // bb, nt),
        in_specs=in_specs, out_specs=out_specs, out_shape=out_shape,
        compiler_params=_cparams(("arbitrary", "arbitrary")),
        name=name,
    )(x3, sc, sh, g, *ws)


def _xnlast_body(x_ref, sc_ref, sh_ref, g_ref, o_ref):
    x = x_ref[...]
    ms = jnp.mean(x * x, axis=-1, keepdims=True)
    xn = x * lax.rsqrt(ms + NORM_EPS) * g_ref[...]
    o_ref[...] = xn * (1.0 + sc_ref[...]) + sh_ref[...]


def _xnlast(x2, sc, sh, g):
    return pl.pallas_call(_xnlast_body, out_shape=jax.ShapeDtypeStruct(x2.shape, f32), name="xn_last")(x2, sc, sh, g)


def _mm_body(x_ref, w_ref, o_ref):
    o_ref[...] = jnp.dot(x_ref[...].astype(bf16), w_ref[...], preferred_element_type=f32)


def _mm(x, w):
    return pl.pallas_call(_mm_body, out_shape=jax.ShapeDtypeStruct((x.shape[0], w.shape[1]), f32),
                          compiler_params=_cparams(None), name="shift_proj")(x, w)


def _rwpre_body(p_ref, prev_ref, mu_ref, w0_ref, w2_ref, a0_ref, a2_ref, g2_ref, kk_ref_, ka_ref, rk_ref, ones_ref,
                r_o, d_o, k_o, v_o, kk_o, b_o, bon_o, g_o):
    p = p_ref[...]
    q = p + (prev_ref[...] - p) * mu_ref[...]
    r = q[:, 0:RW_DIM]
    k = q[:, RW_DIM:2 * RW_DIM]
    v = q[:, 2 * RW_DIM:3 * RW_DIM]
    l2 = q[:, 3 * RW_DIM:3 * RW_DIM + 128]
    lg = q[:, 3 * RW_DIM + 128:3 * RW_DIM + 256]
    ones = ones_ref[...]
    ww = _mmp(jnp.tanh(l2), w2_ref[...], _NN, 3)
    w = -_softplus(-(w0_ref[...] + ww)) - 0.5
    ew = jnp.exp(w)
    aa = _mmp(l2, a2_ref[...], _NN, 3)
    a = _sigmoid(a0_ref[...] + aa)
    g = _mmp(_sigmoid(lg), g2_ref[...], _NN, 3)
    kk = k * kk_ref_[...]
    ss = _headsum(kk * kk, ones)
    kk = kk / jnp.maximum(jnp.sqrt(ss), 1e-12)
    k2 = k * (1.0 + (a - 1.0) * ka_ref[...])
    rk = _headsum(r * k2 * rk_ref[...], ones)
    r_o[...] = r
    d_o[...] = ew
    k_o[...] = k2
    v_o[...] = v
    kk_o[...] = kk
    b_o[...] = kk * a
    bon_o[...] = rk * v
    g_o[...] = g


def _rwpre(p, prev, prm):
    N = p.shape[0]
    tm = math.gcd(N, 256)
    row = lambda c: pl.BlockSpec((tm, c), lambda i: (i, 0))
    full = lambda a: pl.BlockSpec(a.shape, lambda i: (0, 0))
    params = [prm["mu"], prm["w0"], prm["w2p"], prm["a0"], prm["a2p"], prm["g2"], prm["k_k"], prm["k_a"], prm["r_k"],
              prm["ones64"]]
    return pl.pallas_call(
        _rwpre_body,
        grid=(N // tm,),
        in_specs=[row(RW_COLS), row(RW_COLS)] + [full(a) for a in params],
        out_specs=[row(RW_DIM)] * 8,
        out_shape=[jax.ShapeDtypeStruct((N, RW_DIM), f32)] * 8,
        compiler_params=_cparams(("arbitrary",)),
        name="rwkv_pre",
    )(p, prev, *params)


def _segsum(x, ones):
    hi = x.astype(bf16)
    lo = (x - hi.astype(f32)).astype(bf16)
    return (jnp.dot(hi, ones, preferred_element_type=f32) + jnp.dot(lo, ones, preferred_element_type=f32))


def _scan_body(r_ref, d_ref, k_ref, v_ref, kk_ref, b_ref, s0_ref, y_ref, sT_ref, S, pm_s, p2_s, *, bb, Tc, ntc):
    tci = pl.program_id(1)
    npair = RW_HEADS // 2
    nblk = bb * npair
    R = nblk * RW_HEAD

    @pl.when(tci == 0)
    def _():
        for blk in range(nblk):
            b, p = divmod(blk, npair)
            S[blk * 64:(blk + 1) * 64, :] = jnp.concatenate([s0_ref[b, 2 * p], s0_ref[b, 2 * p + 1]], axis=-1)

    ri = lax.broadcasted_iota(jnp.int32, (LANES, LANES), 0)
    ci = lax.broadcasted_iota(jnp.int32, (LANES, LANES), 1)
    ones = jnp.where((ri // 64) == (ci // 64), 1.0, 0.0).astype(bf16)
    r64 = lax.broadcasted_iota(jnp.int32, (RW_HEAD, LANES), 0)
    c64 = lax.broadcasted_iota(jnp.int32, (RW_HEAD, LANES), 1)
    eye2 = jnp.where((c64 % 64) == r64, 1.0, 0.0).astype(f32)

    def step(t, carry):
        row = lambda ref: [ref[b, pl.ds(t, 1), :] for b in range(bb)]
        kk_t, v_t, b_t, k_t, r_t = row(kk_ref), row(v_ref), row(b_ref), row(k_ref), row(r_ref)
        d_t = [jnp.exp(-x) for x in row(d_ref)]
        for blk in range(nblk):
            b, p = divmod(blk, npair)
            rows = slice(blk * 64, (blk + 1) * 64)
            ln = slice(p * LANES, (p + 1) * LANES)
            pm_s[rows, :] = S[rows, :] * kk_t[b][:, ln]
            pm_s[R + blk * 64:R + (blk + 1) * 64, :] = eye2 * v_t[b][:, ln]
        sv = _segsum(pm_s[...], ones)
        for blk in range(nblk):
            b, p = divmod(blk, npair)
            rows = slice(blk * 64, (blk + 1) * 64)
            ln = slice(p * LANES, (p + 1) * LANES)
            sn = (S[rows, :] * d_t[b][:, ln]
                  - sv[blk * 64:(blk + 1) * 64, :] * b_t[b][:, ln]
                  + sv[R + blk * 64:R + (blk + 1) * 64, :] * k_t[b][:, ln])
            S[rows, :] = sn
            p2_s[rows, :] = sn * r_t[b][:, ln]
        yb = _segsum(p2_s[...], ones)
        for b in range(bb):
            yrow = [jnp.sum(eye2 * yb[(b * npair + p) * 64:(b * npair + p + 1) * 64, :], axis=0, keepdims=True)
                    for p in range(npair)]
            y_ref[b, pl.ds(t, 1), :] = jnp.concatenate(yrow, axis=-1)
        return carry

    lax.fori_loop(0, Tc, step, 0)

    @pl.when(tci == ntc - 1)
    def _():
        for blk in range(nblk):
            b, p = divmod(blk, npair)
            s = S[blk * 64:(blk + 1) * 64, :]
            sT_ref[b, 2 * p] = s[:, :64]
            sT_ref[b, 2 * p + 1] = s[:, 64:]


def _scan(r, d, k, v, kk, b, s0):
    B, T, _ = r.shape
    bb = min(SCAN_BB, B)
    Tc = min(64, T)
    ntc = T // Tc
    R = bb * (RW_HEADS // 2) * RW_HEAD
    tok = pl.BlockSpec((bb, Tc, RW_DIM), lambda i, j: (i, j, 0))
    st = pl.BlockSpec((bb, RW_HEADS, RW_HEAD, RW_HEAD), lambda i, j: (i, 0, 0, 0))
    return pl.pallas_call(
        functools.partial(_scan_body, bb=bb, Tc=Tc, ntc=ntc),
        grid=(B // bb, ntc),
        in_specs=[tok] * 6 + [st],
        out_specs=[tok, st],
        out_shape=[jax.ShapeDtypeStruct((B, T, RW_DIM), f32),
                   jax.ShapeDtypeStruct((B, RW_HEADS, RW_HEAD, RW_HEAD), f32)],
        scratch_shapes=[pltpu.VMEM((R, LANES), f32), pltpu.VMEM((2 * R, LANES), f32), pltpu.VMEM((R, LANES), f32)],
        compiler_params=_cparams(("arbitrary", "arbitrary")),
        name="rwkv_scan",
    )(r, d, k, v, kk, b, s0)


RW_CHUNK = 64
RW_CHUNK_BB = 4
RW_PASSES_STATE = 3
RW_PASSES_LOCAL = 1


def _rwchunk_body(r_ref, e_ref, k_ref, v_ref, kk_ref, b_ref, s0_ref, y_ref, sT_ref, sbd, *, C, nck, bb):
    ci = pl.program_id(1)
    npair = RW_HEADS // 2
    H2 = 2 * C
    lane = lax.broadcasted_iota(jnp.int32, (C, LANES), 1)
    m0 = lane < RW_HEAD
    rr = lax.broadcasted_iota(jnp.int32, (H2, H2), 0)
    cc = lax.broadcasted_iota(jnp.int32, (H2, H2), 1)
    same = (rr // C) == (cc // C)
    strict = same & ((cc % C) < (rr % C))
    incl = same & ((cc % C) <= (rr % C))
    r128 = lax.broadcasted_iota(jnp.int32, (LANES, LANES), 0)
    c128 = lax.broadcasted_iota(jnp.int32, (LANES, LANES), 1)
    eye = jnp.where(r128 == c128, 1.0, 0.0).astype(f32)
    blk = (r128 // RW_HEAD) == (c128 // RW_HEAD)

    @pl.when(ci == 0)
    def _():
        z = jnp.zeros((RW_HEAD, RW_HEAD), f32)
        for bi in range(bb):
            for p in range(npair):
                top = jnp.concatenate([s0_ref[bi, 2 * p], z], axis=1)
                bot = jnp.concatenate([z, s0_ref[bi, 2 * p + 1]], axis=1)
                sbd[bi * npair + p] = jnp.concatenate([top, bot], axis=0).T

    tr = lax.broadcasted_iota(jnp.int32, (C, C), 0)
    tc = lax.broadcasted_iota(jnp.int32, (C, C), 1)
    tril = jnp.where(tr >= tc, 1.0, 0.0).astype(f32)
    stack = lambda x: jnp.concatenate([jnp.where(m0, x, 0.0), jnp.where(m0, 0.0, x)], axis=0)
    chains = range(bb * npair)
    ar, bk, v_, pc, kc, bc = [], [], [], [], [], []
    for bp in chains:
        bi, p = divmod(bp, npair)
        if p == 0:
            e_all = e_ref[bi]
            cum_all = jnp.dot(tril, e_all, precision=HI, preferred_element_type=f32)
        ln = slice(p * LANES, (p + 1) * LANES)
        cum = cum_all[:, ln]
        pt = jnp.exp(-cum)
        pi = jnp.exp(cum)
        pm = jnp.exp(e_all[:, ln] - cum)
        b_ = stack(pi * b_ref[bi, :, ln])
        k_ = stack(pi * k_ref[bi, :, ln])
        ar.append(jnp.concatenate([stack(pm * kk_ref[bi, :, ln]), stack(pt * r_ref[bi, :, ln])], axis=0))
        bk.append(jnp.concatenate([b_, k_], axis=0))
        v_.append(stack(v_ref[bi, :, ln]))
        pc.append(pt[C - 1:C, :])
        kc.append(k_ * pc[-1])
        bc.append(b_ * pc[-1])
    big = [_mmp(ar[c], bk[c], _NT, RW_PASSES_LOCAL) for c in chains]
    s = [sbd[c] for c in chains]
    g0 = [_mmp(ar[c], s[c], _NN, RW_PASSES_STATE) for c in chains]
    lk = [jnp.where(strict, big[c][:H2, H2:], 0.0) for c in chains]
    w = [g0[c][:H2] + _mmp(lk[c], v_[c], _NN, RW_PASSES_LOCAL) for c in chains]
    x = [jnp.where(strict, -big[c][:H2, :H2], 0.0) for c in chains]
    nfac = C.bit_length() - 1
    for j in range(nfac):
        w = [w[c] + _mmp(x[c], w[c], _NN, RW_PASSES_LOCAL) for c in chains]
        if j < nfac - 1:
            x = [_mmp(x[c], x[c], _NN, RW_PASSES_LOCAL) for c in chains]
    mk = [jnp.where(incl, big[c][H2:, H2:], 0.0) for c in chains]
    mb = [jnp.where(incl, big[c][H2:, :H2], 0.0) for c in chains]
    ya = [g0[c][H2:] + _mmp(mk[c], v_[c], _NN, RW_PASSES_LOCAL) for c in chains]
    yb = [_mmp(mb[c], w[c], _NN, RW_PASSES_LOCAL) for c in chains]
    up = [_mmp(kc[c], v_[c], _TN, RW_PASSES_STATE) for c in chains]
    um = [_mmp(bc[c], w[c], _TN, RW_PASSES_STATE) for c in chains]
    for c in chains:
        bi, p = divmod(c, npair)
        yy = ya[c] - yb[c]
        y_ref[bi, :, p * LANES:(p + 1) * LANES] = yy[:C] + yy[C:]
        col = jnp.sum(eye * pc[c], axis=1, keepdims=True)
        sbd[c] = s[c] * col + jnp.where(blk, up[c] - um[c], 0.0)

    @pl.when(ci == nck - 1)
    def _():
        for bi in range(bb):
            for p in range(npair):
                m = sbd[bi * npair + p].T
                sT_ref[bi, 2 * p] = m[:RW_HEAD, :RW_HEAD]
                sT_ref[bi, 2 * p + 1] = m[RW_HEAD:, RW_HEAD:]


def _rwchunk(r, e, k, v, kk, b, s0):
    B, T, _ = r.shape
    C = RW_CHUNK
    nck = T // C
    bb = math.gcd(B, RW_CHUNK_BB)
    tok = pl.BlockSpec((bb, C, RW_DIM), lambda i, j: (i, j, 0))
    st = pl.BlockSpec((bb, RW_HEADS, RW_HEAD, RW_HEAD), lambda i, j: (i, 0, 0, 0))
    return pl.pallas_call(
        functools.partial(_rwchunk_body, C=C, nck=nck, bb=bb),
        grid=(B // bb, nck),
        in_specs=[tok] * 6 + [st],
        out_specs=[tok, st],
        out_shape=[jax.ShapeDtypeStruct((B, T, RW_DIM), f32),
                   jax.ShapeDtypeStruct((B, RW_HEADS, RW_HEAD, RW_HEAD), f32)],
        scratch_shapes=[pltpu.VMEM((bb * (RW_HEADS // 2), LANES, LANES), f32)],
        compiler_params=_cparams(("arbitrary", "arbitrary")),
        name="rwkv_chunk",
    )(r, e, k, v, kk, b, s0)


def _mamba_body(z_ref, xbc_ref, dt_ref, dtT_ref, conv0_ref, ssm0_ref, cw_ref, cb_ref, dtb_ref, dtbT_ref, alog_ref,
                alogT_ref, dexp_ref, nw_ref, ex_ref, ob_ref, convT_ref, ssmT_ref, ext, hst, ybuf, *, L, nc):
    c = pl.program_id(1)
    CD = SSM_CONV_DIM

    @pl.when(c == 0)
    def _():
        ext[0:8, :] = jnp.zeros((8, CD), f32)
        ext[5:8, :] = conv0_ref[0]
        hst[...] = ssm0_ref[0].reshape(hst.shape)

    ext[8:8 + L, :] = xbc_ref[...]
    cw = cw_ref[...]
    conv = (cb_ref[...] + ext[5:5 + L, :] * cw[0:1] + ext[6:6 + L, :] * cw[1:2]
            + ext[7:7 + L, :] * cw[2:3] + ext[8:8 + L, :] * cw[3:4])
    xc = conv * _sigmoid(conv)

    @pl.when(c == nc - 1)
    def _():
        convT_ref[0] = ext[L + 5:L + 8, :]

    ext[0:8, :] = ext[L:L + 8, :]

    dtc = _softplus(dt_ref[...] + dtb_ref[...])
    a_col = dtc * (-jnp.exp(alog_ref[...]))
    dtr = _softplus(dtT_ref[0] + dtbT_ref[...])
    a_row = dtr * (-jnp.exp(alogT_ref[...]))
    ri = lax.broadcasted_iota(jnp.int32, (L, L), 0)
    ci = lax.broadcasted_iota(jnp.int32, (L, L), 1)
    lower = ri >= ci
    cs_col = jnp.dot(jnp.where(lower, 1.0, 0.0).astype(f32), a_col, precision=HI, preferred_element_type=f32)
    cs_row = jnp.dot(a_row, jnp.where(ri <= ci, 1.0, 0.0).astype(f32), precision=HI, preferred_element_type=f32)

    xs = xc[:, :SSM_DIM]
    gn = SSM_GROUPS * SSM_STATE
    npair = SSM_HEADS // 2
    ppg = npair // SSM_GROUPS
    ex = ex_ref[...]
    cs_x = jnp.dot(cs_col, ex, precision=HI, preferred_element_type=f32)
    dt_x = jnp.dot(dtc, ex, precision=HI, preferred_element_type=f32)
    ecs = jnp.exp(cs_x)
    xw = (xs * (jnp.exp(cs_x[L - 1:L, :] - cs_x) * dt_x)).astype(bf16)
    m0 = lax.broadcasted_iota(jnp.int32, (L, LANES), 1) < SSM_HEAD
    top = lax.broadcasted_iota(jnp.int32, (LANES, SSM_STATE), 0) < SSM_HEAD
    Bb = [xc[:, SSM_DIM + g * SSM_STATE:SSM_DIM + (g + 1) * SSM_STATE].astype(bf16) for g in range(SSM_GROUPS)]
    Cb = [xc[:, SSM_DIM + gn + g * SSM_STATE:SSM_DIM + gn + (g + 1) * SSM_STATE].astype(bf16)
          for g in range(SSM_GROUPS)]
    cbm = [_nt(Cb[g], Bb[g]) for g in range(SSM_GROUPS)]
    pairs = range(npair)
    lanes = [slice(p * LANES, (p + 1) * LANES) for p in pairs]

    def gmat(h):
        ldec = jnp.exp(jnp.where(lower, cs_col[:, h:h + 1] - cs_row[h:h + 1, :], NEG))
        return (cbm[h // (SSM_HEADS // SSM_GROUPS)] * ldec * dtr[h:h + 1, :]).astype(bf16)

    hp = [hst[p] for p in pairs]
    yo = [_nt(Cb[p // ppg], hp[p].astype(bf16)) for p in pairs]
    up = [_tn(xw[:, lanes[p]], Bb[p // ppg]) for p in pairs]
    x0 = [jnp.where(m0, xs[:, lanes[p]], 0.0).astype(bf16) for p in pairs]
    x1 = [jnp.where(m0, 0.0, xs[:, lanes[p]]).astype(bf16) for p in pairs]
    yd = [jnp.dot(gmat(2 * p), x0[p], preferred_element_type=f32)
          + jnp.dot(gmat(2 * p + 1), x1[p], preferred_element_type=f32) for p in pairs]
    for p in pairs:
        el = jnp.where(top, jnp.exp(cs_row[2 * p:2 * p + 1, L - 1:L]), jnp.exp(cs_row[2 * p + 1:2 * p + 2, L - 1:L]))
        hst[p] = el * hp[p] + up[p]
        ybuf[:, lanes[p]] = yd[p] + ecs[:, lanes[p]] * yo[p]

    z = z_ref[...]
    y = (ybuf[...] + xs * dexp_ref[...]) * (z * _sigmoid(z))
    gw = SSM_DIM // SSM_GROUPS
    for g in range(SSM_GROUPS):
        yg = y[:, g * gw:(g + 1) * gw]
        ms = jnp.mean(yg * yg, axis=-1, keepdims=True)
        ob_ref[:, g * gw:(g + 1) * gw] = yg * lax.rsqrt(ms + NORM_EPS) * nw_ref[:, g * gw:(g + 1) * gw]

    @pl.when(c == nc - 1)
    def _():
        ssmT_ref[0] = hst[...].reshape(SSM_HEADS, SSM_HEAD, SSM_STATE)


def _mamba(z, xbc, dt, dtT, conv0, ssm0, prm, B, T):
    L =math.gcd(T, SSM_CHUNK)
    nc = T // L
    row = lambda cdim: pl.BlockSpec((L, cdim), lambda b, c: (b * nc + c, 0))
    full = lambda a: pl.BlockSpec(a.shape, lambda b, c: (0,) * a.ndim)
    params = [prm["conv_w"], prm["conv_b"], prm["dt_bias"], prm["dt_biasT"], prm["A_log"], prm["A_logT"],
              prm["D_exp"], prm["norm_w"], prm["ssm_expand"]]
    return pl.pallas_call(
        functools.partial(_mamba_body, L=L, nc=nc),
        grid=(B, nc),
        in_specs=[row(SSM_DIM), row(SSM_CONV_DIM), row(LANES),
                  pl.BlockSpec((1, SSM_HEADS, L), lambda b, c: (b, 0, c)),
                  pl.BlockSpec((1, SSM_CONV - 1, SSM_CONV_DIM), lambda b, c: (b, 0, 0)),
                  pl.BlockSpec((1, SSM_HEADS, SSM_HEAD, SSM_STATE), lambda b, c: (b, 0, 0, 0))]
                 + [full(a) for a in params],
        out_specs=[row(SSM_DIM),
                   pl.BlockSpec((1, SSM_CONV - 1, SSM_CONV_DIM), lambda b, c: (b, 0, 0)),
                   pl.BlockSpec((1, SSM_HEADS, SSM_HEAD, SSM_STATE), lambda b, c: (b, 0, 0, 0))],
        out_shape=[jax.ShapeDtypeStruct((B * T, SSM_DIM), f32),
                   jax.ShapeDtypeStruct((B, SSM_CONV - 1, SSM_CONV_DIM), f32),
                   jax.ShapeDtypeStruct((B, SSM_HEADS, SSM_HEAD, SSM_STATE), f32)],
        scratch_shapes=[pltpu.VMEM((L + 8, SSM_CONV_DIM), f32),
                        pltpu.VMEM((SSM_HEADS // 2, 2 * SSM_HEAD, SSM_STATE), f32),
                        pltpu.VMEM((L, SSM_DIM), f32)],
        compiler_params=_cparams(("arbitrary", "arbitrary")),
        name="mamba",
    )(z, xbc, dt, dtT, conv0, ssm0, *params)


def _merge_body(y_ref, bon_ref, g_ref, ob_ref, gate_ref, x_ref, gt1_ref, sc2_ref, sh2_ref, lnw_ref, lnb_ref, ones_ref,
                wpa_ref, wpb_ref, wout_ref, n2g_ref, x1_ref, xn2_ref):
    y = y_ref[...]
    ones = ones_ref[...]
    inv = 1.0 / RW_HEAD
    mean = _headsum(y, ones) * inv
    yc = y - mean
    var = _headsum(yc * yc, ones) * inv
    yn = yc * lax.rsqrt(var + RW_LN_EPS) * lnw_ref[...] + lnb_ref[...]
    oa = (yn + bon_ref[...]) * g_ref[...]
    gate = _sigmoid(gate_ref[...])
    ma = jnp.dot(oa.astype(bf16), wpa_ref[...], preferred_element_type=f32)
    mb = jnp.dot(ob_ref[...].astype(bf16), wpb_ref[...], preferred_element_type=f32)
    merged = gate[:, :D_MODEL] * ma + gate[:, D_MODEL:] * mb
    mo = jnp.dot(merged.astype(bf16), wout_ref[...], preferred_element_type=f32)
    x = x_ref[...]
    bb, tt, Dm = x.shape
    x1 = x + gt1_ref[...] * mo.reshape(bb, tt, Dm)
    x1_ref[...] = x1.reshape(bb * tt, Dm)
    ms = jnp.mean(x1 * x1, axis=-1, keepdims=True)
    xn2 = x1 * lax.rsqrt(ms + NORM_EPS) * n2g_ref[...]
    xn2 = xn2 * (1.0 + sc2_ref[...]) + sh2_ref[...]
    xn2_ref[...] = xn2.reshape(bb * tt, Dm).astype(bf16)


def _merge(y, bon, g, ob, gate, x3, gt1, sc2, sh2, prm):
    B, T, Dm = x3.shape
    bb, tt = _tile_bt(B, T)
    if tt == 512:
        tt = 256
    nt = T // tt
    rows = bb * tt
    row = lambda c: pl.BlockSpec((rows, c), lambda i, j: (i * nt + j, 0))
    mod = pl.BlockSpec((bb, 1, Dm), lambda i, j: (i, 0, 0))
    full = lambda a: pl.BlockSpec(a.shape, lambda i, j: (0,) * a.ndim)
    params = [prm["ln_w"], prm["ln_b"], prm["ones64"], prm["w_pa"], prm["w_pb"], prm["w_out"], prm["norm2_g"]]
    return pl.pallas_call(
        _merge_body,
        grid=(B // bb, nt),
        in_specs=[row(RW_DIM), row(RW_DIM), row(RW_DIM), row(SSM_DIM), row(2 * D_MODEL),
                  pl.BlockSpec((bb, tt, Dm), lambda i, j: (i, j, 0)), mod, mod, mod] + [full(a) for a in params],
        out_specs=[row(Dm), row(Dm)],
        out_shape=[jax.ShapeDtypeStruct((B * T, Dm), f32), jax.ShapeDtypeStruct((B * T, Dm), bf16)],
        compiler_params=_cparams(("arbitrary", "arbitrary")),
        name="merge",
    )(y, bon, g, ob, gate, x3, gt1, sc2, sh2, *params)


def _peer_body(xb_ref, wqT_ref, keys_ref, u_ref, vT_ref, o_ref, qT_s, sc_s, cnt_s, a_s, r2_s, b_s, t_s, cand_s, acc_s,
               w_s, *, ne):
    e = pl.program_id(1)
    xb = xb_ref[...]
    TM = xb.shape[0]
    K = PEER_TOPK
    NK = PEER_NKEYS
    half = PEER_DKEY // 2
    nlt = TM // LANES
    ncand = sum(K // (a + 1) for a in range(K))

    @pl.when(e == 0)
    def _():
        qT_s[...] = _nt(wqT_ref[...], xb)
        cand_s[...] = jnp.full(cand_s.shape, NEG, f32)

        def head(h, carry):
            for c in range(2):
                lo = pl.multiple_of(h * PEER_DKEY + c * half, half)
                qh = qT_s[pl.ds(lo, half), :].astype(bf16)
                sc_s[c] = jnp.dot(keys_ref[h, c], qh, preferred_element_type=f32)
            for lt in range(nlt):
                ln = slice(lt * LANES, (lt + 1) * LANES)
                s1 = sc_s[0, :, ln]
                s2 = sc_s[1, :, ln]
                work = s1
                for it in range(K):
                    m = jnp.max(work, axis=0, keepdims=True)
                    t_s[it:it + 1, ln] = m
                    work = jnp.where(work == m, NEG, work)
                work = s2
                r2 = jnp.full((NK, LANES), float(K), f32)
                for it in range(K):
                    m = jnp.max(work, axis=0, keepdims=True)
                    t_s[K + it:K + it + 1, ln] = m
                    hit = work == m
                    r2 = jnp.where(hit, float(it), r2)
                    work = jnp.where(hit, NEG, work)
                idx = 0
                for a in range(K):
                    nb = K // (a + 1)
                    cand_s[idx:idx + nb, ln] = t_s[a:a + 1, ln] + t_s[K:K + nb, ln]
                    idx += nb
                cand = cand_s[:, ln]
                rank = jnp.zeros(cand.shape, f32)
                for j in range(ncand):
                    rank = rank + jnp.where(cand_s[j:j + 1, ln] > cand, 1.0, 0.0)
                tau = jnp.min(jnp.where(rank <= K - 1.0, cand, -NEG), axis=0, keepdims=True)
                mx = cand_s[0:1, ln]
                zsum = jnp.sum(jnp.where(cand >= tau, jnp.exp(cand - mx), 0.0), axis=0, keepdims=True)
                t1 = t_s[0:K, ln]
                cnt = jnp.zeros((NK, LANES), f32)
                for b in range(K):
                    ok = (t1 + t_s[K + b:K + b + 1, ln]) >= tau
                    theta = jnp.min(jnp.where(ok, t1, -NEG), axis=0, keepdims=True)
                    cnt = cnt + jnp.where(s1 >= theta, 1.0, 0.0)
                cnt_s[h, :, ln] = cnt
                a_s[h, :, ln] = jnp.exp(s1 - t_s[0:1, ln])
                r2_s[h, :, :, ln] = r2.astype(bf16).reshape(NK // 16, 16, LANES)
                bv = jnp.exp(s2 - t_s[K:K + 1, ln]) / zsum
                b_s[h, :, :, ln] = bv.astype(bf16).reshape(NK // 16, 16, LANES)
            return carry

        lax.fori_loop(0, PEER_HEADS, head, 0)
        acc_s[...] = jnp.zeros(acc_s.shape, f32)

    at = _nt(u_ref[...], xb)
    nsub = PEER_EB // NK
    for ii in range(nsub):
        i = e * nsub + ii
        gsum = jnp.zeros((NK // 16, 16, TM), bf16)
        for h in range(PEER_HEADS):
            crow = jnp.broadcast_to(cnt_s[h, pl.ds(i, 1), :], (16, TM)).astype(bf16)[None]
            arow = jnp.broadcast_to(a_s[h, pl.ds(i, 1), :], (16, TM)).astype(bf16)[None]
            gsum = gsum + jnp.where(r2_s[h] < crow, b_s[h], jnp.zeros((), bf16)) * arow
        av = at[ii * NK:(ii + 1) * NK, :]
        act = 0.5 * av * (1.0 + lax.erf(av * INV_SQRT2))
        w_s[ii * NK:(ii + 1) * NK, :] = gsum.reshape(NK, TM) * act.astype(bf16)
    acc_s[...] += jnp.dot(vT_ref[...], w_s[...], preferred_element_type=f32)

    @pl.when(e == ne - 1)
    def _():
        o_ref[...] = acc_s[...].T


def _peer(xb, prm):
    N, Dm = xb.shape
    TM = PEER_TM
    ne = (PEER_NKEYS * PEER_NKEYS) // PEER_EB
    K = PEER_TOPK
    return pl.pallas_call(
        functools.partial(_peer_body, ne=ne),
        grid=(N // TM, ne),
        in_specs=[pl.BlockSpec((TM, Dm), lambda i, e: (i, 0)),
                  pl.BlockSpec(prm["wqT"].shape, lambda i, e: (0, 0)),
                  pl.BlockSpec(prm["keys"].shape, lambda i, e: (0, 0, 0, 0)),
                  pl.BlockSpec((PEER_EB, Dm), lambda i, e: (e, 0)),
                  pl.BlockSpec((Dm, PEER_EB), lambda i, e: (0, e))],
        out_specs=pl.BlockSpec((TM, Dm), lambda i, e: (i, 0)),
        out_shape=jax.ShapeDtypeStruct((N, Dm), f32),
        scratch_shapes=[pltpu.VMEM((PEER_HEADS * PEER_DKEY, TM), f32), pltpu.VMEM((2, PEER_NKEYS, TM), f32),
                        pltpu.VMEM((PEER_HEADS, PEER_NKEYS, TM), f32), pltpu.VMEM((PEER_HEADS, PEER_NKEYS, TM), f32),
                        pltpu.VMEM((PEER_HEADS, PEER_NKEYS // 16, 16, TM), bf16),
                        pltpu.VMEM((PEER_HEADS, PEER_NKEYS // 16, 16, TM), bf16),
                        pltpu.VMEM((2 * K, TM), f32), pltpu.VMEM((56, TM), f32), pltpu.VMEM((Dm, TM), f32),
                        pltpu.VMEM((PEER_EB, TM), bf16)],
        compiler_params=_cparams(("arbitrary", "arbitrary")),
        name="peer",
    )(xb, prm["wqT"], prm["keys"], prm["u"], prm["vT"])


def _final_body(x1_ref, pe_ref, gt2_ref, scf_ref, shf_ref, fg_ref, o_ref):
    bb, tt, Dm = o_ref.shape
    x2 = x1_ref[...].reshape(bb, tt, Dm) + gt2_ref[...] * pe_ref[...].reshape(bb, tt, Dm)
    ms = jnp.mean(x2 * x2, axis=-1, keepdims=True)
    y = x2 * lax.rsqrt(ms + NORM_EPS) * fg_ref[...]
    o_ref[...] = y * (1.0 + scf_ref[...]) + shf_ref[...]


def _final(x1, pe, gt2, scf, shf, fg, B, T):
    Dm = x1.shape[1]
    bb, tt = _tile_bt(B, T)
    nt = T // tt
    row = pl.BlockSpec((bb * tt, Dm), lambda i, j: (i * nt + j, 0))
    mod = pl.BlockSpec((bb, 1, Dm), lambda i, j: (i, 0, 0))
    return pl.pallas_call(
        _final_body,
        grid=(B // bb, nt),
        in_specs=[row, row, mod, mod, mod, pl.BlockSpec((1, 1, Dm), lambda i, j: (0, 0, 0))],
        out_specs=pl.BlockSpec((bb, tt, Dm), lambda i, j: (i, j, 0)),
        out_shape=jax.ShapeDtypeStruct((B, T, Dm), f32),
        compiler_params=_cparams(("arbitrary", "arbitrary")),
        name="final_norm",
    )(x1, pe, gt2, scf, shf, fg)


def _prepare(w_in, rw_mu, rw_w0, rw_w2, rw_a0, rw_a2, rw_g2, rw_k_k, rw_k_a, rw_r_k, rw_ln_w, rw_ln_b, ssm_conv_w,
             ssm_conv_b, ssm_dt_bias, ssm_A_log, ssm_D, ssm_norm_w, w_pa, w_pb, w_out, norm1_g, norm2_g, peer_wq,
             peer_keys, peer_u, peer_v, l):
    w = w_in[l]
    o1 = RW_COLS
    o2 = o1 + SSM_DIM
    o3 = o2 + SSM_CONV_DIM
    o4 = o3 + SSM_HEADS
    seg = lax.broadcasted_iota(jnp.int32, (LANES, LANES), 0) // RW_HEAD
    seg_c = lax.broadcasted_iota(jnp.int32, (LANES, LANES), 1) // RW_HEAD
    zlora = jnp.zeros((64, RW_DIM), f32)
    pad16 = lambda v: jnp.pad(v.reshape(1, SSM_HEADS), ((0, 0), (0, LANES - SSM_HEADS)))
    return dict(
        w_rw=w[:, :o1].astype(bf16), w_z=w[:, o1:o2].astype(bf16), w_xbc=w[:, o2:o3].astype(bf16),
        w_dt=jnp.pad(w[:, o3:o4], ((0, 0), (0, LANES - SSM_HEADS))).astype(bf16), w_gate=w[:, o4:].astype(bf16),
        norm1_g=norm1_g[l].reshape(1, 1, D_MODEL), norm1_g2=norm1_g[l].reshape(1, D_MODEL),
        mu=rw_mu[l].reshape(1, RW_COLS), w0=rw_w0[l].reshape(1, RW_DIM), a0=rw_a0[l].reshape(1, RW_DIM),
        w2p=jnp.concatenate([rw_w2[l], zlora], axis=0), a2p=jnp.concatenate([zlora, rw_a2[l]], axis=0), g2=rw_g2[l],
        k_k=rw_k_k[l].reshape(1, RW_DIM), k_a=rw_k_a[l].reshape(1, RW_DIM), r_k=rw_r_k[l].reshape(1, RW_DIM),
        ln_w=rw_ln_w[l].reshape(1, RW_DIM), ln_b=rw_ln_b[l].reshape(1, RW_DIM),
        ones64=jnp.where(seg == seg_c, 1.0, 0.0).astype(bf16),
        conv_w=ssm_conv_w[l], conv_b=ssm_conv_b[l].reshape(1, SSM_CONV_DIM),
        dt_bias=pad16(ssm_dt_bias[l]), dt_biasT=ssm_dt_bias[l].reshape(SSM_HEADS, 1),
        A_log=pad16(ssm_A_log[l]), A_logT=ssm_A_log[l].reshape(SSM_HEADS, 1),
        ssm_expand=jnp.where(lax.broadcasted_iota(jnp.int32, (LANES, SSM_DIM), 0)
                             == lax.broadcasted_iota(jnp.int32, (LANES, SSM_DIM), 1) // SSM_HEAD, 1.0, 0.0).astype(f32),
        D_exp=jnp.repeat(ssm_D[l], SSM_HEAD).reshape(1, SSM_DIM), norm_w=ssm_norm_w[l].reshape(1, SSM_DIM),
        w_pa=w_pa[l].astype(bf16), w_pb=w_pb[l].astype(bf16), w_out=w_out[l].astype(bf16),
        norm2_g=norm2_g[l].reshape(1, 1, D_MODEL),
        wqT=peer_wq[l].T.astype(bf16), keys=peer_keys[l].astype(bf16), u=peer_u[l].astype(bf16),
        vT=peer_v[l].T.astype(bf16),
    )


def _layer_pre(x, mod, shift0, wkv0, conv0, ssm0, prm):
    B, T, Dm = x.shape
    N = B * T
    m3 = lambda i: mod[:, i * Dm:(i + 1) * Dm].reshape(B, 1, Dm)
    sh1, sc1, gt1, sh2, sc2, gt2 = [m3(i) for i in range(6)]
    p_rw, dt = _nm_matmul(x, sc1, sh1, prm["norm1_g"], [prm["w_rw"], prm["w_dt"]], "in_proj_rw")
    z, xbc = _nm_matmul(x, sc1, sh1, prm["norm1_g"], [prm["w_z"], prm["w_xbc"]], "in_proj_ssm")
    (gate,) = _nm_matmul(x, sc1, sh1, prm["norm1_g"], [prm["w_gate"]], "in_proj_gate")
    xn_last = _xnlast(x[:, -1, :], sc1[:, 0], sh1[:, 0], prm["norm1_g2"])
    p_prev = _mm(shift0, prm["w_rw"])
    p3 = p_rw.reshape(B, T, RW_COLS)
    prev = jnp.concatenate([p_prev[:, None, :], p3[:, :-1, :]], axis=1).reshape(N, RW_COLS)
    r, d, k, v, kk, b, bon, g = _rwpre(p_rw, prev, prm)
    t3 = lambda a: a.reshape(B, T, RW_DIM)
    rw_scan = _rwchunk if T % RW_CHUNK == 0 else _scan
    y, wkvT = rw_scan(t3(r), t3(d), t3(k), t3(v), t3(kk), t3(b), wkv0)
    dtT = jnp.transpose(dt[:, :SSM_HEADS].reshape(B, T, SSM_HEADS), (0, 2, 1))
    ob, convT, ssmT = _mamba(z, xbc, dt, dtT, conv0, ssm0, prm, B, T)
    x1, xn2 = _merge(y.reshape(N, RW_DIM), bon, g, ob, gate, x, gt1, sc2, sh2, prm)
    return x1, xn2, gt2, xn_last, wkvT, convT, ssmT


def _trunk(xs, cs, states, prm, w_ada, b_ada, final_g, w_ada_f, b_ada_f):
    Dm = xs[0].shape[-1]
    nb = [x.shape[0] for x in xs]
    offs = [sum(nb[:i]) for i in range(len(xs) + 1)]
    c_all = jnp.concatenate(cs, axis=0)
    mod = _ada(c_all, w_ada, b_ada.reshape(1, -1))
    fm = _ada(c_all, w_ada_f, b_ada_f.reshape(1, -1))
    pre = [_layer_pre(x, mod[offs[i]:offs[i + 1]], *states[i], prm) for i, x in enumerate(xs)]
    xn2_all = jnp.concatenate([p[1] for p in pre], axis=0)
    n_all = xn2_all.shape[0]
    n_pad = -(-n_all // PEER_TM) * PEER_TM
    if n_pad != n_all:
        xn2_all = jnp.pad(xn2_all, ((0, n_pad - n_all), (0, 0)))
    pe_all = _peer(xn2_all, prm)
    outs = []
    tok = 0
    for i, x in enumerate(xs):
        B, T, _ = x.shape
        x1, _, gt2, xn_last, wkvT, convT, ssmT = pre[i]
        m = fm[offs[i]:offs[i + 1]]
        shf = m[:, :Dm].reshape(B, 1, Dm)
        scf = m[:, Dm:].reshape(B, 1, Dm)
        y = _final(x1, pe_all[tok:tok + B * T], gt2, scf, shf, final_g.reshape(1, 1, Dm), B, T)
        tok += B * T
        outs.append((y, xn_last[None], wkvT[None], convT[None], ssmT[None]))
    return outs


def kernel(x_prompt, x_sample, c_prompt, c_sample, state_shift, state_wkv, state_conv, state_ssm, w_ada, b_ada, norm1_g, w_in, rw_mu, rw_w0, rw_w2, rw_a0, rw_a2, rw_g2, rw_k_k, rw_k_a, rw_r_k, rw_ln_w, rw_ln_b, ssm_conv_w, ssm_conv_b, ssm_dt_bias, ssm_A_log, ssm_D, ssm_norm_w, w_pa, w_pb, w_out, norm2_g, peer_wq, peer_keys, peer_u, peer_v, final_g, w_ada_f, b_ada_f):
    assert w_in.shape[0] == 1, "single-layer trunk"
    Bp = x_prompt.shape[0]
    prm = _prepare(w_in, rw_mu, rw_w0, rw_w2, rw_a0, rw_a2, rw_g2, rw_k_k, rw_k_a, rw_r_k, rw_ln_w, rw_ln_b,
                   ssm_conv_w, ssm_conv_b, ssm_dt_bias, ssm_A_log, ssm_D, ssm_norm_w, w_pa, w_pb, w_out,
                   norm1_g, norm2_g, peer_wq, peer_keys, peer_u, peer_v, 0)
    zeros = (jnp.zeros((Bp,) + state_shift.shape[2:], f32), jnp.zeros((Bp,) + state_wkv.shape[2:], f32),
             jnp.zeros((Bp,) + state_conv.shape[2:], f32), jnp.zeros((Bp,) + state_ssm.shape[2:], f32))
    carried = (state_shift[0], state_wkv[0], state_conv[0], state_ssm[0])
    (yp, *sp), (ys, *ss) = _trunk([x_prompt, x_sample], [c_prompt, c_sample], [zeros, carried], prm,
                                  w_ada[0], b_ada[0], final_g, w_ada_f, b_ada_f)
    return (yp, ys, *sp, *ss)
```
